```python
import jax
import jax.numpy as jnp
from jax import lax
import numpy as np

D_MODEL = 1024
BATCH = 4
SEQ = 8192
DEPTH = 2

N_GROUPS = 4
D_GROUP = D_MODEL // N_GROUPS
HEAD_DIM = 64
N_HEADS = D_GROUP // HEAD_DIM
CHUNK = 64
CONF_KERNEL = 31
SHORT_CONV = 4
ROPE_BASE = 10000.0
D_FF = -(-8 * D_MODEL // (3 * 256)) * 256
IN_SIZES = (D_GROUP, D_GROUP, D_GROUP, D_GROUP,
            D_GROUP, D_GROUP,
            D_GROUP, D_GROUP, D_GROUP, D_GROUP,
            N_HEADS, N_HEADS,
            D_GROUP, D_GROUP, D_GROUP, D_GROUP)
D_IN = 14 * D_GROUP + 2 * N_HEADS

kernel_name = 'hybrid_parallel_head_groups_trunk'


def rms_norm(x, w, eps=1e-6):
    xf = x.astype(jnp.float32)
    y = xf * lax.rsqrt(jnp.mean(xf * xf, axis=-1, keepdims=True) + eps)
    return (y * w.astype(jnp.float32)).astype(x.dtype)


def layer_norm(x, w, b, eps=1e-5):
    mu = jnp.mean(x, axis=-1, keepdims=True)
    var = jnp.mean(jnp.square(x - mu), axis=-1, keepdims=True)
    return (x - mu) * lax.rsqrt(var + eps) * w.astype(jnp.float32) + b.astype(jnp.float32)


def head_rms_norm(o, w, eps=1e-6):
    y = o * lax.rsqrt(jnp.mean(o * o, axis=-1, keepdims=True) + eps) * w.astype(jnp.float32)
    return y.reshape(o.shape[0], o.shape[1], -1)


def head_group_norm(o, w, eps=1e-5):
    mu = jnp.mean(o, axis=-1, keepdims=True)
    var = jnp.mean(jnp.square(o - mu), axis=-1, keepdims=True)
    y = (o - mu) * lax.rsqrt(var + eps) * w.astype(jnp.float32).reshape(N_HEADS, HEAD_DIM)
    return y.reshape(o.shape[0], o.shape[1], -1)


def l2_normalize(x, eps=1e-6):
    return x * lax.rsqrt(jnp.sum(x * x, axis=-1, keepdims=True) + eps)


def split_heads(t):
    b, s, _ = t.shape
    return t.reshape(b, s, -1, HEAD_DIM).transpose(0, 2, 1, 3)


def to_chunks(t):
    b, h, s = t.shape[:3]
    t = t.reshape(b, h, s // CHUNK, CHUNK, *t.shape[3:])
    return jnp.moveaxis(t, 2, 0)


def from_chunks(t):
    n, b, h, c, dv = t.shape
    return jnp.moveaxis(t, 0, 2).reshape(b, h, n * c, dv).transpose(0, 2, 1, 3)


def causal_depthwise_conv(x, w):
    k, c = w.shape
    return lax.conv_general_dilated(
        x, w.astype(x.dtype)[:, None, :], window_strides=(1,), padding=[(k - 1, 0)],
        dimension_numbers=('NWC', 'WIO', 'NWC'), feature_group_count=c)


def apply_rotary(x, pos):
    half = HEAD_DIM // 2
    inv = ROPE_BASE ** (-jnp.arange(half, dtype=jnp.float32) / half)
    ang = pos[:, None] * inv[None, :]
    cos, sin = jnp.cos(ang), jnp.sin(ang)
    x1, x2 = x[..., :half], x[..., half:]
    return jnp.concatenate([x1 * cos - x2 * sin, x1 * sin + x2 * cos], axis=-1)


def retention_mixer(q, k, v):
    q, k, v = split_heads(q), split_heads(k), split_heads(v)
    pos = jnp.arange(q.shape[2], dtype=jnp.float32)
    q = apply_rotary(q, pos)
    k = apply_rotary(k, pos) * HEAD_DIM ** -0.5
    log_gamma = jnp.log1p(-(2.0 ** (-5.0 - jnp.arange(N_HEADS, dtype=jnp.float32))))
    idx = jnp.arange(CHUNK, dtype=jnp.float32)
    diff = idx[:, None] - idx[None, :]
    decay_mask = jnp.where(diff >= 0, jnp.exp(jnp.maximum(diff, 0.0) * log_gamma[:, None, None]), 0.0)
    q_decay = jnp.exp((idx + 1.0) * log_gamma[:, None])[..., None]
    k_decay = jnp.exp((CHUNK - 1.0 - idx) * log_gamma[:, None])[..., None]
    chunk_decay = jnp.exp(CHUNK * log_gamma)[:, None, None]

    def step(state, inp):
        qc, kc, vc = inp
        scores = jnp.einsum('bhid,bhjd->bhij', qc, kc) * decay_mask
        out = (jnp.einsum('bhij,bhjv->bhiv', scores, vc)
               + jnp.einsum('bhid,bhdv->bhiv', qc * q_decay, state))
        state = state * chunk_decay + jnp.einsum('bhjd,bhjv->bhdv', kc * k_decay, vc)
        return state, out

    s0 = jnp.zeros((q.shape[0], N_HEADS, HEAD_DIM, HEAD_DIM), jnp.float32)
    _, out = lax.scan(step, s0, (to_chunks(q), to_chunks(k), to_chunks(v)))
    return from_chunks(out)


def conformer_conv_mixer(a, gate, conv_w, conv_b, ln_w, ln_b):
    glu = a * jax.nn.sigmoid(gate)
    y = causal_depthwise_conv(glu, conv_w) + conv_b.astype(jnp.float32)
    return jax.nn.silu(layer_norm(y, ln_w, ln_b))


def gated_deltanet_mixer(q, k, v, beta_logit, a_logit, A_log, dt_bias):
    q = l2_normalize(split_heads(q)) * HEAD_DIM ** -0.5
    k = l2_normalize(split_heads(k))
    v = split_heads(v)
    beta = jax.nn.sigmoid(beta_logit).transpose(0, 2, 1)
    g = (-jnp.exp(A_log.astype(jnp.float32))
         * jax.nn.softplus(a_logit + dt_bias.astype(jnp.float32))).transpose(0, 2, 1)
    idx = jnp.arange(CHUNK)
    incl = idx[:, None] >= idx[None, :]
    strict = idx[:, None] > idx[None, :]

    def step(state, inp):
        qc, kc, vc, bc, gc = inp
        G = jnp.cumsum(gc, axis=-1)
        L = jnp.exp(jnp.where(incl, G[..., :, None] - G[..., None, :], -jnp.inf))
        kb = kc * bc[..., None]
        A = jnp.where(strict, jnp.einsum('bhid,bhjd->bhij', kb, kc) * L, 0.0)
        rhs = jnp.concatenate([vc * bc[..., None], kb * jnp.exp(G)[..., None]], axis=-1)
        sol = lax.linalg.triangular_solve(A, rhs, left_side=True, lower=True, unit_diagonal=True)
        u, w = sol[..., :HEAD_DIM], sol[..., HEAD_DIM:]
        v_new = u - jnp.einsum('bhik,bhkv->bhiv', w, state)
        attn = jnp.einsum('bhid,bhjd->bhij', qc, kc) * L
        out = (jnp.einsum('bhid,bhdv->bhiv', qc * jnp.exp(G)[..., None], state)
               + jnp.einsum('bhij,bhjv->bhiv', attn, v_new))
        g_last = G[..., -1:]
        state = (state * jnp.exp(g_last)[..., None]
                 + jnp.einsum('bhjd,bhjv->bhdv', kc * jnp.exp(g_last - G)[..., None], v_new))
        return state, out

    s0 = jnp.zeros((q.shape[0], N_HEADS, HEAD_DIM, HEAD_DIM), jnp.float32)
    _, out = lax.scan(step, s0, (to_chunks(q), to_chunks(k), to_chunks(v), to_chunks(beta), to_chunks(g)))
    return from_chunks(out)


def hgrn2_mixer(q, f_logit, i, lb):
    lb = lb.astype(jnp.float32)
    log_f = jnp.logaddexp(jnp.log(lb), jnp.log1p(-lb) + jax.nn.log_sigmoid(f_logit))
    k = -jnp.expm1(log_f)
    q, k, v, gf = split_heads(q), split_heads(k), split_heads(i), split_heads(log_f)
    idx = jnp.arange(CHUNK)
    incl = (idx[:, None] >= idx[None, :])[:, :, None]

    def step(state, inp):
        qc, kc, vc, gc = inp
        G = jnp.cumsum(gc, axis=2)
        dec = jnp.exp(jnp.where(incl, G[:, :, :, None, :] - G[:, :, None, :, :], -jnp.inf))
        scores = jnp.einsum('bhid,bhijd,bhjd->bhij', qc, dec, kc)
        out = (jnp.einsum('bhij,bhjv->bhiv', scores, vc)
               + jnp.einsum('bhid,bhdv->bhiv', qc * jnp.exp(G), state))
        g_last = G[:, :, -1:, :]
        state = (state * jnp.exp(g_last[:, :, 0, :])[..., None]
                 + jnp.einsum('bhjd,bhjv->bhdv', kc * jnp.exp(g_last - G), vc))
        return state, out

    s0 = jnp.zeros((q.shape[0], N_HEADS, HEAD_DIM, HEAD_DIM), jnp.float32)
    _, out = lax.scan(step, s0, (to_chunks(q), to_chunks(k), to_chunks(v), to_chunks(gf)))
    return from_chunks(out)


def setup_inputs(seed: int = 0) -> dict:
    key = jax.random.key(seed)
    ks = jax.random.split(key, 20)
    f32 = jnp.float32
    nrm = lambda k, shape, scale: jax.random.normal(k, shape, f32) * scale
    gain = lambda k, shape: 1.0 + 0.02 * jax.random.normal(k, shape, f32)
    dt = jnp.exp(jax.random.uniform(ks[10], (DEPTH, N_HEADS), f32, minval=np.log(1e-3), maxval=np.log(1e-1)))
    return {
        'x': jax.random.normal(ks[0], (BATCH, SEQ, D_MODEL), f32),
        'norm_mix_w': gain(ks[1], (DEPTH, D_MODEL)),
        'w_in': nrm(ks[2], (DEPTH, D_MODEL, D_IN), D_MODEL ** -0.5),
        'ret_norm_w': gain(ks[3], (DEPTH, D_GROUP)),
        'conf_conv_w': nrm(ks[4], (DEPTH, CONF_KERNEL, D_GROUP), CONF_KERNEL ** -0.5),
        'conf_conv_b': nrm(ks[5], (DEPTH, D_GROUP), 0.02),
        'conf_ln_w': gain(ks[6], (DEPTH, D_GROUP)),
        'conf_ln_b': nrm(ks[7], (DEPTH, D_GROUP), 0.02),
        'gdn_conv_w': nrm(ks[8], (DEPTH, SHORT_CONV, 3 * D_GROUP), SHORT_CONV ** -0.5),
        'gdn_A_log': jnp.log(jax.random.uniform(ks[9], (DEPTH, N_HEADS), f32, minval=1.0, maxval=16.0)),
        'gdn_dt_bias': dt + jnp.log(-jnp.expm1(-dt)),
        'gdn_norm_w': gain(ks[11], (DEPTH, HEAD_DIM)),
        'hgrn_lb_logits': nrm(ks[12], (DEPTH, D_GROUP), 0.1),
        'hgrn_norm_w': gain(ks[13], (DEPTH, HEAD_DIM)),
        'w_out': nrm(ks[14], (DEPTH, D_MODEL, D_MODEL), D_MODEL ** -0.5),
        'norm_ffn_w': gain(ks[15], (DEPTH, D_MODEL)),
        'ffn_w_gate': nrm(ks[16], (DEPTH, D_MODEL, D_FF), D_MODEL ** -0.5),
        'ffn_w_up': nrm(ks[17], (DEPTH, D_MODEL, D_FF), D_MODEL ** -0.5),
        'ffn_w_down': nrm(ks[18], (DEPTH, D_FF, D_MODEL), D_FF ** -0.5),
        'final_norm_w': gain(ks[19], (D_MODEL,)),
    }


def reference(x, norm_mix_w, w_in, ret_norm_w, conf_conv_w, conf_conv_b, conf_ln_w, conf_ln_b,
              gdn_conv_w, gdn_A_log, gdn_dt_bias, gdn_norm_w, hgrn_lb_logits, hgrn_norm_w,
              w_out, norm_ffn_w, ffn_w_gate, ffn_w_up, ffn_w_down, final_norm_w):
    b, t = x.shape[0], x.shape[1]
    offsets = [int(o) for o in np.cumsum(IN_SIZES)[:-1]]
    lb_all = jnp.cumsum(jax.nn.softmax(hgrn_lb_logits.astype(jnp.float32), axis=0), axis=0)
    lb_all = lb_all - lb_all[0:1]
    for l in range(DEPTH):
        h = rms_norm(x, norm_mix_w[l])
        proj = (h @ w_in[l]).astype(jnp.float32)
        (rq, rk, rv, rg, ca, cg, gq, gk, gv, gg, gb, ga, hq, hf, hi, hgate) = jnp.split(proj, offsets, axis=-1)
        ret = head_group_norm(retention_mixer(rq, rk, rv), ret_norm_w[l]) * jax.nn.silu(rg)
        conf = conformer_conv_mixer(ca, cg, conf_conv_w[l], conf_conv_b[l], conf_ln_w[l], conf_ln_b[l])
        qkv = jax.nn.silu(causal_depthwise_conv(jnp.concatenate([gq, gk, gv], axis=-1), gdn_conv_w[l]))
        gq, gk, gv = jnp.split(qkv, [D_GROUP, 2 * D_GROUP], axis=-1)
        gdn = head_rms_norm(gated_deltanet_mixer(gq, gk, gv, gb, ga, gdn_A_log[l], gdn_dt_bias[l]),
                            gdn_norm_w[l]) * jax.nn.silu(gg)
        hg = head_rms_norm(hgrn2_mixer(hq, hf, hi, lb_all[l]), hgrn_norm_w[l]) * jax.nn.silu(hgate)
        mix = jnp.concatenate([ret, conf, gdn, hg], axis=-1).astype(x.dtype)
        x = x + mix @ w_out[l]
        h = rms_norm(x, norm_ffn_w[l])
        x = x + (jax.nn.silu(h @ ffn_w_gate[l]) * (h @ ffn_w_up[l])) @ ffn_w_down[l]
    return rms_norm(x, final_norm_w)
```

```python
import functools

import numpy as np
import jax
import jax.numpy as jnp
from jax import lax
from jax.experimental import pallas as pl
from jax.experimental.pallas import tpu as pltpu

F32 = jnp.float32
BF16 = jnp.bfloat16

D_MODEL = 1024
DEPTH = 2
D_GROUP = 256
HEAD_DIM = 64
N_HEADS = 4
CHUNK = 64
SUB = 16
CONF_KERNEL = 31
CONF_PAD = 32
SHORT_CONV = 4
GDN_PAD = 8
ROPE_BASE = 10000.0
D_FF = 2816
FF_CHUNKS = ((0, 1536), (1536, D_FF))
MIX_TBLK = 512
FFN_TM = 512
VMEM_LIMIT = 56 * 1024 * 1024

_LOG_GAMMA = [float(np.log1p(-(2.0 ** (-5.0 - h)))) for h in range(N_HEADS)]


def _iota(shape, dim):
    return lax.broadcasted_iota(jnp.int32, shape, dim)


def _dot(a, b):
    return jnp.dot(a, b, preferred_element_type=F32)


def _dot_nt(a, b):
    return lax.dot_general(a, b, (((1,), (1,)), ((), ())), preferred_element_type=F32)


def _dot_tn(a, b):
    return lax.dot_general(a, b, (((0,), (0,)), ((), ())), preferred_element_type=F32)


def _sigmoid(x):
    return 1.0 / (1.0 + jnp.exp(-x))


def _silu(x):
    return x * _sigmoid(x)


def _softplus(z):
    return jnp.maximum(z, 0.0) + jnp.log1p(jnp.exp(-jnp.abs(z)))


def _split_hi_lo(x):
    hi = x.astype(BF16)
    lo = (x - hi.astype(F32)).astype(BF16)
    return hi, lo


def _dot_exact_rhs(x, m):
    hi, lo = _split_hi_lo(x)
    return _dot(hi, m) + _dot(lo, m)


def _dot_exact_lhs(m, x):
    hi, lo = _split_hi_lo(x)
    return _dot(m, hi) + _dot(m, lo)


def _head_block_mask(rows_per_head_r, n):
    return (_iota((n, D_GROUP), 0) // rows_per_head_r) == (_iota((n, D_GROUP), 1) // HEAD_DIM)


def _stack_heads(x16, mask):
    return jnp.where(mask, jnp.concatenate([x16] * N_HEADS, axis=0), jnp.zeros((), x16.dtype))


def _select_by_head(hid, vals):
    out = jnp.full(hid.shape, vals[N_HEADS - 1], F32)
    for h in range(N_HEADS - 2, -1, -1):
        out = jnp.where(hid == h, vals[h], out)
    return out


def _mixer_call(body, *, batch, seq, n_cols, small_inputs, scratch, h, w, extra_time_inputs=()):
    tblk = min(MIX_TBLK, seq)
    assert seq % tblk == 0
    in_specs = [
        pl.BlockSpec((1, tblk, D_MODEL), lambda b, t: (b, t, 0)),
        pl.BlockSpec((D_MODEL, n_cols), lambda b, t: (0, 0)),
    ]
    for a in extra_time_inputs:
        in_specs.append(pl.BlockSpec((tblk, a.shape[1]), lambda b, t: (t, 0)))
    for a in small_inputs:
        in_specs.append(pl.BlockSpec(a.shape, lambda b, t: (0, 0)))
    return pl.pallas_call(
        functools.partial(body, tblk=tblk),
        grid=(batch, seq // tblk),
        in_specs=in_specs,
        out_specs=pl.BlockSpec((1, tblk, D_GROUP), lambda b, t: (b, t, 0)),
        out_shape=jax.ShapeDtypeStruct((batch, seq, D_GROUP), BF16),
        scratch_shapes=[pltpu.VMEM((tblk, n_cols), F32)] + scratch(tblk),
        compiler_params=pltpu.CompilerParams(
            dimension_semantics=("parallel", "arbitrary"), vmem_limit_bytes=VMEM_LIMIT),
    )(h, w, *extra_time_inputs, *small_inputs)


def _rmsnorm_kernel(x_ref, w_ref, o_ref):
    x = x_ref[...]
    y = x * lax.rsqrt(jnp.mean(x * x, axis=-1, keepdims=True) + 1e-6) * w_ref[...]
    o_ref[...] = y.astype(o_ref.dtype)


def _rmsnorm(x2d, w, out_dtype):
    n = x2d.shape[0]
    tm = min(1024, n)
    return pl.pallas_call(
        _rmsnorm_kernel,
        grid=(n // tm,),
        in_specs=[pl.BlockSpec((tm, D_MODEL), lambda i: (i, 0)),
                  pl.BlockSpec((1, D_MODEL), lambda i: (0, 0))],
        out_specs=pl.BlockSpec((tm, D_MODEL), lambda i: (i, 0)),
        out_shape=jax.ShapeDtypeStruct((n, D_MODEL), out_dtype),
        compiler_params=pltpu.CompilerParams(dimension_semantics=("parallel",)),
    )(x2d, w.reshape(1, D_MODEL))


def _ret_kernel(h_ref, w_ref, cos_ref, sin_ref, gnw_ref, o_ref, p_ref, q_ref, k_ref, s_ref, *, tblk):
    @pl.when(pl.program_id(1) == 0)
    def _():
        s_ref[...] = jnp.zeros_like(s_ref)

    p_ref[...] = _dot(h_ref[0], w_ref[...])
    cos = cos_ref[...]
    sin = sin_ref[...]
    half = D_GROUP // 2
    for off, dst, scale in ((0, q_ref, 1.0), (D_GROUP, k_ref, HEAD_DIM ** -0.5)):
        x1 = p_ref[:, off:off + half]
        x2 = p_ref[:, off + half:off + D_GROUP]
        dst[:, 0:half] = (x1 * cos - x2 * sin) * scale
        dst[:, half:D_GROUP] = (x1 * sin + x2 * cos) * scale

    c = CHUNK
    lane = _iota((c, D_GROUP), 1)
    row = _iota((c, D_GROUP), 0)
    head_qk = (lane % half) // (HEAD_DIM // 2)
    head_v = lane // HEAD_DIM
    lg_qk = _select_by_head(head_qk, _LOG_GAMMA)
    lg_v = _select_by_head(head_v, _LOG_GAMMA)
    rowf = row.astype(F32)
    diff = row - (lane % c)
    decay_mask = jnp.where(diff >= 0, jnp.exp(jnp.maximum(diff, 0).astype(F32) * lg_v), 0.0)
    q_decay = jnp.exp((rowf + 1.0) * lg_qk)
    k_decay = jnp.exp((c - 1.0 - rowf) * lg_qk)
    chunk_decay = jnp.exp(float(c) * lg_v[0:1, :])

    r4 = _iota((N_HEADS * c, D_GROUP), 0) // c
    l4 = _iota((N_HEADS * c, D_GROUP), 1)
    mask_k = r4 == (l4 % half) // (HEAD_DIM // 2)
    mask_v = r4 == l4 // HEAD_DIM
    sr = _iota((D_GROUP, D_GROUP), 0)
    sl = _iota((D_GROUP, D_GROUP), 1)
    mask_s = (sr % half) // (HEAD_DIM // 2) == sl // HEAD_DIM
    hs = jnp.where(sr // HEAD_DIM == sl // HEAD_DIM, 1.0, 0.0).astype(BF16)
    gnw = gnw_ref[...]

    def body(ci, carry):
        r0 = pl.multiple_of(ci * c, c)
        q = q_ref[pl.ds(r0, c), :]
        k = k_ref[pl.ds(r0, c), :]
        v16 = p_ref[pl.ds(r0, c), 2 * D_GROUP:3 * D_GROUP].astype(BF16)
        gate = p_ref[pl.ds(r0, c), 3 * D_GROUP:4 * D_GROUP]
        kst = _stack_heads(k.astype(BF16), mask_k)
        scores = _dot_nt(q.astype(BF16), kst) * decay_mask
        s = s_ref[...]
        o = (_dot(scores.astype(BF16), _stack_heads(v16, mask_v))
             + _dot((q * q_decay).astype(BF16), s.astype(BF16)))
        s_ref[...] = s * chunk_decay + jnp.where(mask_s, _dot_tn((k * k_decay).astype(BF16), v16), 0.0)
        mu = _dot_exact_rhs(o, hs) * (1.0 / HEAD_DIM)
        d = o - mu
        var = _dot_exact_rhs(d * d, hs) * (1.0 / HEAD_DIM)
        y = d * lax.rsqrt(var + 1e-5) * gnw * _silu(gate)
        o_ref[0, pl.ds(r0, c), :] = y.astype(o_ref.dtype)
        return carry

    lax.fori_loop(0, tblk // c, body, 0)


def _retention(h, w, cos, sin, gnw):
    b, t, _ = h.shape
    scratch = lambda tblk: [pltpu.VMEM((tblk, D_GROUP), F32), pltpu.VMEM((tblk, D_GROUP), F32),
                            pltpu.VMEM((D_GROUP, D_GROUP), F32)]
    return _mixer_call(_ret_kernel, batch=b, seq=t, n_cols=4 * D_GROUP, small_inputs=[gnw],
                       scratch=scratch, h=h, w=w, extra_time_inputs=(cos, sin))


def _conf_kernel(h_ref, w_ref, cw_ref, cb_ref, lnw_ref, lnb_ref, o_ref, p_ref, buf_ref, *, tblk):
    @pl.when(pl.program_id(1) == 0)
    def _():
        buf_ref[0:CONF_PAD, :] = jnp.zeros((CONF_PAD, D_GROUP), F32)

    p_ref[...] = _dot(h_ref[0], w_ref[...])
    buf_ref[CONF_PAD:CONF_PAD + tblk, :] = p_ref[:, 0:D_GROUP] * _sigmoid(p_ref[:, D_GROUP:2 * D_GROUP])
    cw = cw_ref[...]
    cb = cb_ref[...]
    lnw = lnw_ref[...]
    lnb = lnb_ref[...]
    c = CHUNK
    first = CONF_PAD - (CONF_KERNEL - 1)

    def body(ci, carry):
        r0 = pl.multiple_of(ci * c, c)
        win = buf_ref[pl.ds(r0, c + CONF_PAD), :]
        acc = jnp.zeros((c, D_GROUP), F32)
        for tap in range(CONF_KERNEL):
            acc = acc + cw[tap:tap + 1, :] * win[first + tap:first + tap + c, :]
        y = acc + cb
        mu = jnp.mean(y, axis=-1, keepdims=True)
        d = y - mu
        var = jnp.mean(d * d, axis=-1, keepdims=True)
        z = d * lax.rsqrt(var + 1e-5) * lnw + lnb
        o_ref[0, pl.ds(r0, c), :] = _silu(z).astype(o_ref.dtype)
        return carry

    lax.fori_loop(0, tblk // c, body, 0)
    buf_ref[0:CONF_PAD, :] = buf_ref[tblk:tblk + CONF_PAD, :]


def _conformer(h, w, cw, cb, lnw, lnb):
    b, t, _ = h.shape
    scratch = lambda tblk: [pltpu.VMEM((tblk + CONF_PAD, D_GROUP), F32)]
    return _mixer_call(_conf_kernel, batch=b, seq=t, n_cols=2 * D_GROUP,
                       small_inputs=[cw, cb, lnw, lnb], scratch=scratch, h=h, w=w)


def _gdn_kernel(h_ref, w_ref, cw_ref, nega_ref, dtb_ref, nw_ref, o_ref, p_ref, buf_ref, s_ref, *, tblk):
    @pl.when(pl.program_id(1) == 0)
    def _():
        s_ref[...] = jnp.zeros_like(s_ref)
        buf_ref[0:GDN_PAD, :] = jnp.zeros((GDN_PAD, 3 * D_GROUP), F32)

    p_ref[...] = _dot(h_ref[0], w_ref[...])
    buf_ref[GDN_PAD:GDN_PAD + tblk, :] = p_ref[:, 0:3 * D_GROUP]
    cw = cw_ref[...]
    nega = nega_ref[...]
    dtb = dtb_ref[...]
    nw = nw_ref[...]
    c = CHUNK
    first = GDN_PAD - (SHORT_CONV - 1)

    lane = _iota((c, D_GROUP), 1)
    row = _iota((c, D_GROUP), 0)
    col_j = lane % c
    incl = row >= col_j
    strict = row > col_j
    eye = row == col_j
    eye_f = jnp.where(eye, 1.0, 0.0)
    bm = _head_block_mask(c, N_HEADS * c)
    hs = jnp.where(bm, 1.0, 0.0).astype(BF16)
    tri = jnp.where(_iota((c, c), 0) >= _iota((c, c), 1), 1.0, 0.0).astype(BF16)

    def bd(x):
        return _stack_heads(x.astype(BF16), bm)

    def body(ci, carry):
        r0 = pl.multiple_of(ci * c, c)
        win = buf_ref[pl.ds(r0, c + GDN_PAD), :]
        acc = jnp.zeros((c, 3 * D_GROUP), F32)
        for tap in range(SHORT_CONV):
            acc = acc + cw[tap:tap + 1, :] * win[first + tap:first + tap + c, :]
        qkv = _silu(acc)
        q = qkv[:, 0:D_GROUP]
        k = qkv[:, D_GROUP:2 * D_GROUP]
        v = qkv[:, 2 * D_GROUP:3 * D_GROUP]
        q = q * lax.rsqrt(_dot_exact_rhs(q * q, hs) + 1e-6) * HEAD_DIM ** -0.5
        k = k * lax.rsqrt(_dot_exact_rhs(k * k, hs) + 1e-6)
        gate = p_ref[pl.ds(r0, c), 3 * D_GROUP:4 * D_GROUP]
        beta = _sigmoid(p_ref[pl.ds(r0, c), 4 * D_GROUP:5 * D_GROUP])
        g = nega * _softplus(p_ref[pl.ds(r0, c), 5 * D_GROUP:6 * D_GROUP] + dtb)
        G = _dot_exact_lhs(tri, g)
        eG = jnp.exp(G)
        g_row = jnp.sum(jnp.where(eye, G, 0.0), axis=0, keepdims=True)
        L = jnp.where(incl, jnp.exp(jnp.minimum(G - g_row, 0.0)), 0.0)
        kb = k * beta
        kst = bd(k)
        aa = _dot_nt(jnp.concatenate([kb, q], axis=0).astype(BF16), kst)
        A = jnp.where(strict, aa[0:c] * L, 0.0)
        attn = aa[c:2 * c] * L
        T = eye_f - A
        P = _dot(A.astype(BF16), bd(A))
        for it in range(5):
            tp = _dot(jnp.concatenate([T, P], axis=0).astype(BF16), bd(P))
            T = T + tp[0:c]
            P = tp[c:2 * c]
        T16 = T.astype(BF16)
        u = _dot(T16, bd(v * beta))
        w = _dot(T16, bd(kb * eG))
        s = s_ref[...]
        ws = _dot(jnp.concatenate([w, q * eG], axis=0).astype(BF16), s.astype(BF16))
        v_new = u - ws[0:c]
        o = ws[c:2 * c] + _dot(attn.astype(BF16), bd(v_new))
        g_last = G[c - 1:c, :]
        k_dec = k * jnp.exp(g_last - G)
        s_ref[...] = s * jnp.exp(g_last) + jnp.where(
            bm, _dot_tn(k_dec.astype(BF16), v_new.astype(BF16)), 0.0)
        ms = _dot_exact_rhs(o * o, hs) * (1.0 / HEAD_DIM)
        y = o * lax.rsqrt(ms + 1e-6) * nw * _silu(gate)
        o_ref[0, pl.ds(r0, c), :] = y.astype(o_ref.dtype)
        return carry

    lax.fori_loop(0, tblk // c, body, 0)
    buf_ref[0:GDN_PAD, :] = buf_ref[tblk:tblk + GDN_PAD, :]


def _gdn(h, w, cw, nega, dtb, nw):
    b, t, _ = h.shape
    scratch = lambda tblk: [pltpu.VMEM((tblk + GDN_PAD, 3 * D_GROUP), F32),
                            pltpu.VMEM((D_GROUP, D_GROUP), F32)]
    return _mixer_call(_gdn_kernel, batch=b, seq=t, n_cols=6 * D_GROUP,
                       small_inputs=[cw, nega, dtb, nw], scratch=scratch, h=h, w=w)


def _hgrn_kernel(h_ref, w_ref, loglb_ref, l1m_ref, oml_ref, nw_ref, o_ref, p_ref, st_ref, *, tblk):
    @pl.when(pl.program_id(1) == 0)
    def _():
        st_ref[...] = jnp.zeros_like(st_ref)

    p_ref[...] = _dot(h_ref[0], w_ref[...])
    log_lb = loglb_ref[...]
    l1m = l1m_ref[...]
    oml = oml_ref[...]
    nw = nw_ref[...]
    grp = D_GROUP
    nsub = grp // SUB
    bm = _head_block_mask(HEAD_DIM, D_GROUP)
    hs = jnp.where(bm, 1.0, 0.0).astype(BF16)
    rr = _iota((grp, grp), 0)
    cc = _iota((grp, grp), 1)
    block_tri = jnp.where((rr // SUB == cc // SUB) & (rr >= cc), 1.0, 0.0).astype(BF16)
    row_in_sub = _iota((nsub, SUB, D_GROUP), 1)

    def body(gi, carry):
        r0 = pl.multiple_of(gi * grp, grp)
        q = p_ref[pl.ds(r0, grp), 0:D_GROUP]
        x = p_ref[pl.ds(r0, grp), D_GROUP:2 * D_GROUP]
        v = p_ref[pl.ds(r0, grp), 2 * D_GROUP:3 * D_GROUP]
        gate = p_ref[pl.ds(r0, grp), 3 * D_GROUP:4 * D_GROUP]
        b_ = l1m + (jnp.minimum(x, 0.0) - jnp.log1p(jnp.exp(-jnp.abs(x))))
        log_f = jnp.maximum(log_lb, b_) + jnp.log1p(jnp.exp(-jnp.abs(log_lb - b_)))
        k = oml * _sigmoid(-x)
        G = _dot_exact_lhs(block_tri, log_f)
        G3 = G.reshape(nsub, SUB, D_GROUP)
        q3 = q.reshape(nsub, SUB, D_GROUP)
        k3 = k.reshape(nsub, SUB, D_GROUP)
        v3 = v.reshape(nsub, SUB, D_GROUP)
        intra = jnp.zeros((nsub, SUB, D_GROUP), F32)
        for j in range(SUB):
            e = jnp.exp(jnp.minimum(G3 - G3[:, j:j + 1, :], 0.0))
            z = jnp.where(row_in_sub >= j, q3 * e * k3[:, j:j + 1, :], 0.0)
            sc = _dot(z.reshape(grp, D_GROUP).astype(BF16), hs)
            intra = intra + sc.reshape(nsub, SUB, D_GROUP) * v3[:, j:j + 1, :]
        eG = jnp.exp(G)
        qg = (q * eG).astype(BF16)
        k_end = (k3 * jnp.exp(G3[:, SUB - 1:SUB, :] - G3)).reshape(grp, D_GROUP).astype(BF16)
        v16 = v.astype(BF16)
        intra = intra.reshape(grp, D_GROUP)
        st = st_ref[...]
        outs = []
        for bi in range(nsub):
            lo = bi * SUB
            outs.append(intra[lo:lo + SUB] + _dot_nt(qg[lo:lo + SUB], st.astype(BF16)))
            st = st * eG[lo + SUB - 1:lo + SUB, :] + jnp.where(
                bm, _dot_tn(v16[lo:lo + SUB], k_end[lo:lo + SUB]), 0.0)
        st_ref[...] = st
        o = jnp.concatenate(outs, axis=0)
        ms = _dot_exact_rhs(o * o, hs) * (1.0 / HEAD_DIM)
        y = o * lax.rsqrt(ms + 1e-6) * nw * _silu(gate)
        o_ref[0, pl.ds(r0, grp), :] = y.astype(o_ref.dtype)
        return carry

    lax.fori_loop(0, tblk // grp, body, 0)


def _hgrn(h, w, log_lb, l1m, oml, nw):
    b, t, _ = h.shape
    scratch = lambda tblk: [pltpu.VMEM((D_GROUP, D_GROUP), F32)]
    return _mixer_call(_hgrn_kernel, batch=b, seq=t, n_cols=4 * D_GROUP,
                       small_inputs=[log_lb, l1m, oml, nw], scratch=scratch, h=h, w=w)


def _ffn_kernel(x_ref, m0_ref, m1_ref, m2_ref, m3_ref, wo_ref, nfw_ref, wg_ref, wu_ref, wd_ref, nxt_ref,
                *out_refs, last):
    y = x_ref[...]
    for g, m_ref in enumerate((m0_ref, m1_ref, m2_ref, m3_ref)):
        y = y + _dot(m_ref[...], wo_ref[g * D_GROUP:(g + 1) * D_GROUP, :])
    h = (y * lax.rsqrt(jnp.mean(y * y, axis=-1, keepdims=True) + 1e-6) * nfw_ref[...]).astype(BF16)
    down = None
    for f0, f1 in FF_CHUNKS:
        gate = _dot(h, wg_ref[:, f0:f1])
        up = _dot(h, wu_ref[:, f0:f1])
        part = _dot((_silu(gate) * up).astype(BF16), wd_ref[f0:f1, :])
        down = part if down is None else down + part
    y = y + down
    normed = y * lax.rsqrt(jnp.mean(y * y, axis=-1, keepdims=True) + 1e-6) * nxt_ref[...]
    if last:
        out_refs[0][...] = normed
    else:
        out_refs[0][...] = y
        out_refs[1][...] = normed.astype(BF16)


def _out_ffn(x2d, mixes, wo, nfw, wg, wu, wd, nxt_w, last):
    n = x2d.shape[0]
    tm = min(FFN_TM, n)
    row = lambda cols: pl.BlockSpec((tm, cols), lambda i: (i, 0))
    const = lambda shape: pl.BlockSpec(shape, lambda i: (0, 0), pipeline_mode=pl.Buffered(1))
    in_specs = ([row(D_MODEL)] + [row(D_GROUP)] * 4
                + [const((D_MODEL, D_MODEL)), const((1, D_MODEL)), const((D_MODEL, D_FF)),
                   const((D_MODEL, D_FF)), const((D_FF, D_MODEL)), const((1, D_MODEL))])
    if last:
        out_specs = [row(D_MODEL)]
        out_shape = [jax.ShapeDtypeStruct((n, D_MODEL), F32)]
    else:
        out_specs = [row(D_MODEL), row(D_MODEL)]
        out_shape = [jax.ShapeDtypeStruct((n, D_MODEL), F32), jax.ShapeDtypeStruct((n, D_MODEL), BF16)]
    return pl.pallas_call(
        functools.partial(_ffn_kernel, last=last),
        grid=(n // tm,),
        in_specs=in_specs, out_specs=out_specs, out_shape=out_shape,
        compiler_params=pltpu.CompilerParams(
            dimension_semantics=("parallel",), vmem_limit_bytes=VMEM_LIMIT),
    )(x2d, *mixes, wo, nfw, wg, wu, wd, nxt_w)


def _rope_perm(w):
    d = w.shape[0]
    return w.reshape(d, N_HEADS, 2, HEAD_DIM // 2).transpose(0, 2, 1, 3).reshape(d, D_GROUP)


def _per_head_to_lanes(a):
    return jnp.repeat(a, HEAD_DIM, axis=-1)


def kernel(x, norm_mix_w, w_in, ret_norm_w, conf_conv_w, conf_conv_b, conf_ln_w, conf_ln_b, gdn_conv_w, gdn_A_log, gdn_dt_bias, gdn_norm_w, hgrn_lb_logits, hgrn_norm_w, w_out, norm_ffn_w, ffn_w_gate, ffn_w_up, ffn_w_down, final_norm_w):
    b, t, d = x.shape
    n = b * t
    g = D_GROUP
    row = lambda a: a.reshape(1, -1).astype(F32)

    pos = jnp.arange(t, dtype=F32)
    half = HEAD_DIM // 2
    inv = ROPE_BASE ** (-jnp.arange(half, dtype=F32) / half)
    ang = pos[:, None] * inv[None, :]
    cos = jnp.tile(jnp.cos(ang), (1, N_HEADS))
    sin = jnp.tile(jnp.sin(ang), (1, N_HEADS))

    lb_all = jnp.cumsum(jax.nn.softmax(hgrn_lb_logits.astype(F32), axis=0), axis=0)
    lb_all = lb_all - lb_all[0:1]

    x2d = x.reshape(n, d)
    h = _rmsnorm(x2d, norm_mix_w[0], BF16)
    out = None
    for l in range(DEPTH):
        wl = w_in[l]
        o_gb = 10 * g
        w_ret = jnp.concatenate([_rope_perm(wl[:, 0:g]), _rope_perm(wl[:, g:2 * g]), wl[:, 2 * g:4 * g]],
                                axis=1).astype(BF16)
        w_conf = wl[:, 4 * g:6 * g].astype(BF16)
        w_gdn = jnp.concatenate([wl[:, 6 * g:10 * g],
                                 _per_head_to_lanes(wl[:, o_gb:o_gb + N_HEADS]),
                                 _per_head_to_lanes(wl[:, o_gb + N_HEADS:o_gb + 2 * N_HEADS])],
                                axis=1).astype(BF16)
        w_hgrn = wl[:, o_gb + 2 * N_HEADS:].astype(BF16)

        h3 = h.reshape(b, t, d)
        m_ret = _retention(h3, w_ret, cos, sin, row(ret_norm_w[l]))
        cw = jnp.concatenate([conf_conv_w[l], jnp.zeros((CONF_PAD - CONF_KERNEL, g), F32)], axis=0)
        m_conf = _conformer(h3, w_conf, cw, row(conf_conv_b[l]), row(conf_ln_w[l]), row(conf_ln_b[l]))
        gcw = jnp.concatenate([gdn_conv_w[l], jnp.zeros((GDN_PAD - SHORT_CONV, 3 * g), F32)], axis=0)
        m_gdn = _gdn(h3, w_gdn, gcw,
                     row(_per_head_to_lanes(-jnp.exp(gdn_A_log[l].astype(F32)))),
                     row(_per_head_to_lanes(gdn_dt_bias[l].astype(F32))),
                     row(jnp.tile(gdn_norm_w[l], N_HEADS)))
        lb = lb_all[l]
        m_hgrn = _hgrn(h3, w_hgrn, row(jnp.log(lb)), row(jnp.log1p(-lb)), row(1.0 - lb),
                       row(jnp.tile(hgrn_norm_w[l], N_HEADS)))

        last = l == DEPTH - 1
        nxt_w = final_norm_w if last else norm_mix_w[l + 1]
        mixes = [m.reshape(n, g) for m in (m_ret, m_conf, m_gdn, m_hgrn)]
        res = _out_ffn(x2d, mixes, w_out[l].astype(BF16), row(norm_ffn_w[l]),
                       ffn_w_gate[l].astype(BF16), ffn_w_up[l].astype(BF16), ffn_w_down[l].astype(BF16),
                       row(nxt_w), last)
        if last:
            out = res[0]
        else:
            x2d, h = res
    return out.reshape(b, t, d)
```

```python
import functools

import numpy as np
import jax
import jax.numpy as jnp
from jax import lax
from jax.experimental import pallas as pl
from jax.experimental.pallas import tpu as pltpu

F32 = jnp.float32
BF16 = jnp.bfloat16

D_MODEL = 1024
DEPTH = 2
D_GROUP = 256
HEAD_DIM = 64
N_HEADS = 4
CHUNK = 64
SUB = 16
CONF_KERNEL = 31
CONF_PAD = 32
SHORT_CONV = 4
GDN_PAD = 8
ROPE_BASE = 10000.0
D_FF = 2816
FF_CHUNKS = ((0, 1536), (1536, D_FF))
MIX_TBLK = 512
FFN_TM = 512
VMEM_LIMIT = 56 * 1024 * 1024

_LOG_GAMMA = [float(np.log1p(-(2.0 ** (-5.0 - h)))) for h in range(N_HEADS)]


def _iota(shape, dim):
    return lax.broadcasted_iota(jnp.int32, shape, dim)


def _dot(a, b):
    return jnp.dot(a, b, preferred_element_type=F32)


def _dot_nt(a, b):
    return lax.dot_general(a, b, (((1,), (1,)), ((), ())), preferred_element_type=F32)


def _dot_tn(a, b):
    return lax.dot_general(a, b, (((0,), (0,)), ((), ())), preferred_element_type=F32)


def _sigmoid(x):
    return 1.0 / (1.0 + jnp.exp(-x))


def _silu(x):
    return x * _sigmoid(x)


def _softplus(z):
    return jnp.maximum(z, 0.0) + jnp.log1p(jnp.exp(-jnp.abs(z)))


def _split_hi_lo(x):
    hi = x.astype(BF16)
    lo = (x - hi.astype(F32)).astype(BF16)
    return hi, lo


def _dot_exact_rhs(x, m):
    hi, lo = _split_hi_lo(x)
    return _dot(hi, m) + _dot(lo, m)


def _dot_exact_lhs(m, x):
    hi, lo = _split_hi_lo(x)
    return _dot(m, hi) + _dot(m, lo)


def _head_block_mask(rows_per_head_r, n):
    return (_iota((n, D_GROUP), 0) // rows_per_head_r) == (_iota((n, D_GROUP), 1) // HEAD_DIM)


def _stack_heads(x16, mask):
    return jnp.where(mask, jnp.concatenate([x16] * N_HEADS, axis=0), jnp.zeros((), x16.dtype))


def _select_by_head(hid, vals):
    out = jnp.full(hid.shape, vals[N_HEADS - 1], F32)
    for h in range(N_HEADS - 2, -1, -1):
        out = jnp.where(hid == h, vals[h], out)
    return out


def _mixer_call(body, *, batch, seq, n_cols, small_inputs, scratch, h, w, extra_time_inputs=()):
    tblk = min(MIX_TBLK, seq)
    assert seq % tblk == 0
    in_specs = [
        pl.BlockSpec((1, tblk, D_MODEL), lambda b, t: (b, t, 0)),
        pl.BlockSpec((D_MODEL, n_cols), lambda b, t: (0, 0)),
    ]
    for a in extra_time_inputs:
        in_specs.append(pl.BlockSpec((tblk, a.shape[1]), lambda b, t: (t, 0)))
    for a in small_inputs:
        in_specs.append(pl.BlockSpec(a.shape, lambda b, t: (0, 0)))
    return pl.pallas_call(
        functools.partial(body, tblk=tblk),
        name=body.__name__.strip("_"),
        grid=(batch, seq // tblk),
        in_specs=in_specs,
        out_specs=pl.BlockSpec((1, tblk, D_GROUP), lambda b, t: (b, t, 0)),
        out_shape=jax.ShapeDtypeStruct((batch, seq, D_GROUP), BF16),
        scratch_shapes=[pltpu.VMEM((tblk, n_cols), F32)] + scratch(tblk),
        compiler_params=pltpu.CompilerParams(
            dimension_semantics=("parallel", "arbitrary"), vmem_limit_bytes=VMEM_LIMIT),
    )(h, w, *extra_time_inputs, *small_inputs)


def _rmsnorm_kernel(x_ref, w_ref, o_ref):
    x = x_ref[...]
    y = x * lax.rsqrt(jnp.mean(x * x, axis=-1, keepdims=True) + 1e-6) * w_ref[...]
    o_ref[...] = y.astype(o_ref.dtype)


def _rmsnorm(x2d, w, out_dtype):
    n = x2d.shape[0]
    tm = min(1024, n)
    return pl.pallas_call(
        _rmsnorm_kernel,
        name="rmsnorm",
        grid=(n // tm,),
        in_specs=[pl.BlockSpec((tm, D_MODEL), lambda i: (i, 0)),
                  pl.BlockSpec((1, D_MODEL), lambda i: (0, 0))],
        out_specs=pl.BlockSpec((tm, D_MODEL), lambda i: (i, 0)),
        out_shape=jax.ShapeDtypeStruct((n, D_MODEL), out_dtype),
        compiler_params=pltpu.CompilerParams(dimension_semantics=("parallel",)),
    )(x2d, w.reshape(1, D_MODEL))


def _ret_kernel(h_ref, w_ref, cos_ref, sin_ref, gnw_ref, o_ref, p_ref, s_ref, *, tblk):
    @pl.when(pl.program_id(1) == 0)
    def _():
        s_ref[...] = jnp.zeros_like(s_ref)

    p_ref[...] = _dot(h_ref[0], w_ref[...])
    c = CHUNK
    nc = tblk // c
    half = D_GROUP // 2
    lane = _iota((c, D_GROUP), 1)
    row = _iota((c, D_GROUP), 0)
    head_qk = (lane % half) // (HEAD_DIM // 2)
    head_v = lane // HEAD_DIM
    lg_qk = _select_by_head(head_qk, _LOG_GAMMA)
    lg_v = _select_by_head(head_v, _LOG_GAMMA)
    rowf = row.astype(F32)
    diff = row - (lane % c)
    decay_mask = jnp.where(diff >= 0, jnp.exp(jnp.maximum(diff, 0).astype(F32) * lg_v), 0.0)
    q_decay = jnp.exp((rowf + 1.0) * lg_qk)
    k_decay = jnp.exp((c - 1.0 - rowf) * lg_qk) * HEAD_DIM ** -0.5
    chunk_decay = jnp.exp(float(c) * lg_v[0:1, :])

    r4 = _iota((N_HEADS * c, D_GROUP), 0) // c
    l4 = _iota((N_HEADS * c, D_GROUP), 1)
    mask_k = r4 == (l4 % half) // (HEAD_DIM // 2)
    mask_v = r4 == l4 // HEAD_DIM
    sr = _iota((D_GROUP, D_GROUP), 0)
    sl = _iota((D_GROUP, D_GROUP), 1)
    mask_s = (sr % half) // (HEAD_DIM // 2) == sl // HEAD_DIM
    hs = jnp.where(sr // HEAD_DIM == sl // HEAD_DIM, 1.0, 0.0).astype(BF16)

    def rope(off, r0):
        x1 = p_ref[r0:r0 + c, off:off + half]
        x2 = p_ref[r0:r0 + c, off + half:off + D_GROUP]
        cos = cos_ref[r0:r0 + c, :]
        sin = sin_ref[r0:r0 + c, :]
        return jnp.concatenate([x1 * cos - x2 * sin, x1 * sin + x2 * cos], axis=1)

    rows = [ci * c for ci in range(nc)]
    q = [rope(0, r0) for r0 in rows]
    k = [rope(D_GROUP, r0) for r0 in rows]
    v16 = [p_ref[r0:r0 + c, 2 * D_GROUP:3 * D_GROUP].astype(BF16) for r0 in rows]
    kst = [_stack_heads((k[i] * HEAD_DIM ** -0.5).astype(BF16), mask_k) for i in range(nc)]
    scores = [_dot_nt(q[i].astype(BF16), kst[i]) * decay_mask for i in range(nc)]
    intra = [_dot(scores[i].astype(BF16), _stack_heads(v16[i], mask_v)) for i in range(nc)]
    kv = [jnp.where(mask_s, _dot_tn((k[i] * k_decay).astype(BF16), v16[i]), 0.0) for i in range(nc)]
    qd = [(q[i] * q_decay).astype(BF16) for i in range(nc)]
    s = s_ref[...]
    outs = []
    for i in range(nc):
        outs.append(intra[i] + _dot(qd[i], s.astype(BF16)))
        s = s * chunk_decay + kv[i]
    s_ref[...] = s
    o = jnp.concatenate(outs, axis=0)
    mu = _dot_exact_rhs(o, hs) * (1.0 / HEAD_DIM)
    d = o - mu
    var = _dot_exact_rhs(d * d, hs) * (1.0 / HEAD_DIM)
    y = d * lax.rsqrt(var + 1e-5) * gnw_ref[...] * _silu(p_ref[:, 3 * D_GROUP:4 * D_GROUP])
    o_ref[0] = y.astype(o_ref.dtype)


def _retention(h, w, cos, sin, gnw):
    b, t, _ = h.shape
    scratch = lambda tblk: [pltpu.VMEM((D_GROUP, D_GROUP), F32)]
    return _mixer_call(_ret_kernel, batch=b, seq=t, n_cols=4 * D_GROUP, small_inputs=[gnw],
                       scratch=scratch, h=h, w=w, extra_time_inputs=(cos, sin))


def _conf_kernel(h_ref, w_ref, cw_ref, cb_ref, lnw_ref, lnb_ref, o_ref, p_ref, buf_ref, *, tblk):
    @pl.when(pl.program_id(1) == 0)
    def _():
        buf_ref[0:CONF_PAD, :] = jnp.zeros((CONF_PAD, D_GROUP), F32)

    p_ref[...] = _dot(h_ref[0], w_ref[...])
    buf_ref[CONF_PAD:CONF_PAD + tblk, :] = p_ref[:, 0:D_GROUP] * _sigmoid(p_ref[:, D_GROUP:2 * D_GROUP])
    cw = cw_ref[...]
    first = CONF_PAD - (CONF_KERNEL - 1)
    acc = None
    for r in range(8):
        taps = [tap for tap in range(CONF_KERNEL) if (first + tap) % 8 == r]
        shifted = buf_ref[r:first + taps[-1] + tblk, :]
        for tap in taps:
            a0 = first + tap - r
            term = cw[tap:tap + 1, :] * shifted[a0:a0 + tblk, :]
            acc = term if acc is None else acc + term
    buf_ref[0:CONF_PAD, :] = buf_ref[tblk:tblk + CONF_PAD, :]
    y = acc + cb_ref[...]
    mu = jnp.mean(y, axis=-1, keepdims=True)
    d = y - mu
    var = jnp.mean(d * d, axis=-1, keepdims=True)
    z = d * lax.rsqrt(var + 1e-5) * lnw_ref[...] + lnb_ref[...]
    o_ref[0] = _silu(z).astype(o_ref.dtype)


def _conformer(h, w, cw, cb, lnw, lnb):
    b, t, _ = h.shape
    scratch = lambda tblk: [pltpu.VMEM((tblk + CONF_PAD, D_GROUP), F32)]
    return _mixer_call(_conf_kernel, batch=b, seq=t, n_cols=2 * D_GROUP,
                       small_inputs=[cw, cb, lnw, lnb], scratch=scratch, h=h, w=w)


def _gdn_kernel(h_ref, w_ref, cw_ref, nega_ref, dtb_ref, nw_ref, o_ref, p_ref, buf_ref, s_ref, *, tblk):
    @pl.when(pl.program_id(1) == 0)
    def _():
        s_ref[...] = jnp.zeros_like(s_ref)
        buf_ref[0:GDN_PAD, :] = jnp.zeros((GDN_PAD, 3 * D_GROUP), F32)

    p_ref[...] = _dot(h_ref[0], w_ref[...])
    buf_ref[GDN_PAD:GDN_PAD + tblk, :] = p_ref[:, 0:3 * D_GROUP]
    cw = cw_ref[...]
    c = CHUNK
    nc = tblk // c
    grp = D_GROUP
    first = GDN_PAD - (SHORT_CONV - 1)

    lane = _iota((c, D_GROUP), 1)
    row = _iota((c, D_GROUP), 0)
    col_j = lane % c
    incl = row >= col_j
    strict = row > col_j
    eye = row == col_j
    eye_f = jnp.where(eye, 1.0, 0.0)
    bm = _head_block_mask(c, N_HEADS * c)
    hs = jnp.where(bm, 1.0, 0.0).astype(BF16)
    rr = _iota((grp, grp), 0)
    cc = _iota((grp, grp), 1)
    block_tri = jnp.where((rr // c == cc // c) & (rr >= cc), 1.0, 0.0).astype(BF16)

    def bd(x):
        return _stack_heads(x.astype(BF16), bm)

    acc = None
    for tap in range(SHORT_CONV):
        term = cw[tap:tap + 1, :] * buf_ref[first + tap:first + tap + tblk, :]
        acc = term if acc is None else acc + term
    buf_ref[0:GDN_PAD, :] = buf_ref[tblk:tblk + GDN_PAD, :]
    qkv = _silu(acc)
    q = qkv[:, 0:D_GROUP]
    k = qkv[:, D_GROUP:2 * D_GROUP]
    v = qkv[:, 2 * D_GROUP:3 * D_GROUP]
    q = q * lax.rsqrt(_dot_exact_rhs(q * q, hs) + 1e-6) * HEAD_DIM ** -0.5
    k = k * lax.rsqrt(_dot_exact_rhs(k * k, hs) + 1e-6)
    beta = _sigmoid(p_ref[:, 4 * D_GROUP:5 * D_GROUP])
    g = nega_ref[...] * _softplus(p_ref[:, 5 * D_GROUP:6 * D_GROUP] + dtb_ref[...])
    G = jnp.concatenate([_dot_exact_lhs(block_tri, g[i * grp:(i + 1) * grp]) for i in range(tblk // grp)],
                        axis=0)
    eG = jnp.exp(G)
    kb = k * beta
    vb16 = (v * beta).astype(BF16)
    kbe16 = (kb * eG).astype(BF16)
    qe = q * eG
    k16 = k.astype(BF16)
    lhs_aa = jnp.concatenate([kb.reshape(nc, c, D_GROUP), q.reshape(nc, c, D_GROUP)], axis=1).astype(BF16)

    sl = [slice(i * c, (i + 1) * c) for i in range(nc)]
    Gc = [G[sl[i]] for i in range(nc)]
    g_last = [Gc[i][c - 1:c, :] for i in range(nc)]
    L = []
    for i in range(nc):
        g_row = jnp.sum(jnp.where(eye, Gc[i], 0.0), axis=0, keepdims=True)
        L.append(jnp.where(incl, jnp.exp(jnp.minimum(Gc[i] - g_row, 0.0)), 0.0))
    aa = [_dot_nt(lhs_aa[i], _stack_heads(k16[sl[i]], bm)) for i in range(nc)]
    A = [jnp.where(strict, aa[i][0:c] * L[i], 0.0) for i in range(nc)]
    attn = [(aa[i][c:2 * c] * L[i]).astype(BF16) for i in range(nc)]
    T = [eye_f - A[i] for i in range(nc)]
    P = [_dot(A[i].astype(BF16), bd(A[i])) for i in range(nc)]
    for it in range(5):
        if it < 4:
            tp = [_dot(jnp.concatenate([T[i], P[i]], axis=0).astype(BF16), bd(P[i])) for i in range(nc)]
            T = [T[i] + tp[i][0:c] for i in range(nc)]
            P = [tp[i][c:2 * c] for i in range(nc)]
        else:
            T = [T[i] + _dot(T[i].astype(BF16), bd(P[i])) for i in range(nc)]
    T16 = [T[i].astype(BF16) for i in range(nc)]
    u = [_dot(T16[i], _stack_heads(vb16[sl[i]], bm)) for i in range(nc)]
    w = [_dot(T16[i], _stack_heads(kbe16[sl[i]], bm)) for i in range(nc)]
    lhs_ws = [jnp.concatenate([w[i], qe[sl[i]]], axis=0).astype(BF16) for i in range(nc)]
    k_dec = [(k[sl[i]] * jnp.exp(g_last[i] - Gc[i])).astype(BF16) for i in range(nc)]
    e_last = [jnp.exp(g_last[i]) for i in range(nc)]

    s = s_ref[...]
    outs = []
    for i in range(nc):
        ws = _dot(lhs_ws[i], s.astype(BF16))
        v_new = u[i] - ws[0:c]
        outs.append(ws[c:2 * c] + _dot(attn[i], bd(v_new)))
        s = s * e_last[i] + jnp.where(bm, _dot_tn(k_dec[i], v_new.astype(BF16)), 0.0)
    s_ref[...] = s

    o = jnp.concatenate(outs, axis=0)
    ms = _dot_exact_rhs(o * o, hs) * (1.0 / HEAD_DIM)
    y = o * lax.rsqrt(ms + 1e-6) * nw_ref[...] * _silu(p_ref[:, 3 * D_GROUP:4 * D_GROUP])
    o_ref[0] = y.astype(o_ref.dtype)


def _gdn(h, w, cw, nega, dtb, nw):
    b, t, _ = h.shape
    scratch = lambda tblk: [pltpu.VMEM((tblk + GDN_PAD, 3 * D_GROUP), F32),
                            pltpu.VMEM((D_GROUP, D_GROUP), F32)]
    return _mixer_call(_gdn_kernel, batch=b, seq=t, n_cols=6 * D_GROUP,
                       small_inputs=[cw, nega, dtb, nw], scratch=scratch, h=h, w=w)


def _hgrn_kernel(h_ref, w_ref, loglb_ref, l1m_ref, oml_ref, nw_ref, o_ref, p_ref, st_ref, *, tblk):
    @pl.when(pl.program_id(1) == 0)
    def _():
        st_ref[...] = jnp.zeros_like(st_ref)

    p_ref[...] = _dot(h_ref[0], w_ref[...])
    log_lb = loglb_ref[...]
    l1m = l1m_ref[...]
    oml = oml_ref[...]
    grp = D_GROUP
    nsub = grp // SUB
    bm = _head_block_mask(HEAD_DIM, D_GROUP)
    hs = jnp.where(bm, 1.0, 0.0).astype(BF16)
    rr = _iota((grp, grp), 0)
    cc = _iota((grp, grp), 1)
    block_tri = jnp.where((rr // SUB == cc // SUB) & (rr >= cc), 1.0, 0.0).astype(BF16)
    row_in_sub = _iota((nsub, SUB, D_GROUP), 1)

    intra_all, qg_all, kv_all, e_end_all = [], [], [], []
    for gi in range(tblk // grp):
        r0 = gi * grp
        q = p_ref[r0:r0 + grp, 0:D_GROUP]
        x = p_ref[r0:r0 + grp, D_GROUP:2 * D_GROUP]
        v = p_ref[r0:r0 + grp, 2 * D_GROUP:3 * D_GROUP]
        b_ = l1m + (jnp.minimum(x, 0.0) - jnp.log1p(jnp.exp(-jnp.abs(x))))
        log_f = jnp.maximum(log_lb, b_) + jnp.log1p(jnp.exp(-jnp.abs(log_lb - b_)))
        k = oml * _sigmoid(-x)
        G = _dot_exact_lhs(block_tri, log_f)
        G3 = G.reshape(nsub, SUB, D_GROUP)
        q3 = q.reshape(nsub, SUB, D_GROUP)
        k3 = k.reshape(nsub, SUB, D_GROUP)
        v3 = v.reshape(nsub, SUB, D_GROUP)
        intra = None
        for j in range(SUB):
            e = jnp.exp(jnp.where(row_in_sub >= j, G3 - G3[:, j:j + 1, :], -1e30))
            z = q3 * e * k3[:, j:j + 1, :]
            sc = _dot(z.reshape(grp, D_GROUP).astype(BF16), hs)
            term = sc.reshape(nsub, SUB, D_GROUP) * v3[:, j:j + 1, :]
            intra = term if intra is None else intra + term
        intra = intra.reshape(grp, D_GROUP)
        eG = jnp.exp(G)
        qg = (q * eG).astype(BF16)
        k_end = (k3 * jnp.exp(G3[:, SUB - 1:SUB, :] - G3)).reshape(grp, D_GROUP).astype(BF16)
        v16 = v.astype(BF16)
        for bi in range(nsub):
            lo = bi * SUB
            intra_all.append(intra[lo:lo + SUB])
            qg_all.append(qg[lo:lo + SUB])
            e_end_all.append(eG[lo + SUB - 1:lo + SUB, :])
            kv_all.append(jnp.where(bm, _dot_tn(v16[lo:lo + SUB], k_end[lo:lo + SUB]), 0.0))

    st = st_ref[...]
    outs = []
    for bi in range(len(kv_all)):
        outs.append(intra_all[bi] + _dot_nt(qg_all[bi], st.astype(BF16)))
        st = st * e_end_all[bi] + kv_all[bi]
    st_ref[...] = st
    o = jnp.concatenate(outs, axis=0)
    ms = _dot_exact_rhs(o * o, hs) * (1.0 / HEAD_DIM)
    y = o * lax.rsqrt(ms + 1e-6) * nw_ref[...] * _silu(p_ref[:, 3 * D_GROUP:4 * D_GROUP])
    o_ref[0] = y.astype(o_ref.dtype)


def _hgrn(h, w, log_lb, l1m, oml, nw):
    b, t, _ = h.shape
    scratch = lambda tblk: [pltpu.VMEM((D_GROUP, D_GROUP), F32)]
    return _mixer_call(_hgrn_kernel, batch=b, seq=t, n_cols=4 * D_GROUP,
                       small_inputs=[log_lb, l1m, oml, nw], scratch=scratch, h=h, w=w)


def _ffn_kernel(x_ref, m0_ref, m1_ref, m2_ref, m3_ref, wo_ref, nfw_ref, wg_ref, wu_ref, wd_ref, nxt_ref,
                *out_refs, last):
    y = x_ref[...]
    for g, m_ref in enumerate((m0_ref, m1_ref, m2_ref, m3_ref)):
        y = y + _dot(m_ref[...], wo_ref[g * D_GROUP:(g + 1) * D_GROUP, :])
    h = (y * lax.rsqrt(jnp.mean(y * y, axis=-1, keepdims=True) + 1e-6) * nfw_ref[...]).astype(BF16)
    down = None
    for f0, f1 in FF_CHUNKS:
        gate = _dot(h, wg_ref[:, f0:f1])
        up = _dot(h, wu_ref[:, f0:f1])
        part = _dot((_silu(gate) * up).astype(BF16), wd_ref[f0:f1, :])
        down = part if down is None else down + part
    y = y + down
    normed = y * lax.rsqrt(jnp.mean(y * y, axis=-1, keepdims=True) + 1e-6) * nxt_ref[...]
    if last:
        out_refs[0][...] = normed
    else:
        out_refs[0][...] = y
        out_refs[1][...] = normed.astype(BF16)


def _out_ffn(x2d, mixes, wo, nfw, wg, wu, wd, nxt_w, last):
    n = x2d.shape[0]
    tm = min(FFN_TM, n)
    row = lambda cols: pl.BlockSpec((tm, cols), lambda i: (i, 0))
    const = lambda shape: pl.BlockSpec(shape, lambda i: (0, 0), pipeline_mode=pl.Buffered(1))
    in_specs = ([row(D_MODEL)] + [row(D_GROUP)] * 4
                + [const((D_MODEL, D_MODEL)), const((1, D_MODEL)), const((D_MODEL, D_FF)),
                   const((D_MODEL, D_FF)), const((D_FF, D_MODEL)), const((1, D_MODEL))])
    if last:
        out_specs = [row(D_MODEL)]
        out_shape = [jax.ShapeDtypeStruct((n, D_MODEL), F32)]
    else:
        out_specs = [row(D_MODEL), row(D_MODEL)]
        out_shape = [jax.ShapeDtypeStruct((n, D_MODEL), F32), jax.ShapeDtypeStruct((n, D_MODEL), BF16)]
    return pl.pallas_call(
        functools.partial(_ffn_kernel, last=last),
        name="out_ffn_last" if last else "out_ffn",
        grid=(n // tm,),
        in_specs=in_specs, out_specs=out_specs, out_shape=out_shape,
        compiler_params=pltpu.CompilerParams(
            dimension_semantics=("parallel",), vmem_limit_bytes=VMEM_LIMIT),
    )(x2d, *mixes, wo, nfw, wg, wu, wd, nxt_w)


def _rope_perm(w):
    d = w.shape[0]
    return w.reshape(d, N_HEADS, 2, HEAD_DIM // 2).transpose(0, 2, 1, 3).reshape(d, D_GROUP)


def _per_head_to_lanes(a):
    return jnp.repeat(a, HEAD_DIM, axis=-1)


def kernel(x, norm_mix_w, w_in, ret_norm_w, conf_conv_w, conf_conv_b, conf_ln_w, conf_ln_b, gdn_conv_w, gdn_A_log, gdn_dt_bias, gdn_norm_w, hgrn_lb_logits, hgrn_norm_w, w_out, norm_ffn_w, ffn_w_gate, ffn_w_up, ffn_w_down, final_norm_w):
    b, t, d = x.shape
    n = b * t
    g = D_GROUP
    row = lambda a: a.reshape(1, -1).astype(F32)

    pos = jnp.arange(t, dtype=F32)
    half = HEAD_DIM // 2
    inv = ROPE_BASE ** (-jnp.arange(half, dtype=F32) / half)
    ang = pos[:, None] * inv[None, :]
    cos = jnp.tile(jnp.cos(ang), (1, N_HEADS))
    sin = jnp.tile(jnp.sin(ang), (1, N_HEADS))

    lb_all = jnp.cumsum(jax.nn.softmax(hgrn_lb_logits.astype(F32), axis=0), axis=0)
    lb_all = lb_all - lb_all[0:1]

    x2d = x.reshape(n, d)
    h = _rmsnorm(x2d, norm_mix_w[0], BF16)
    out = None
    for l in range(DEPTH):
        wl = w_in[l]
        o_gb = 10 * g
        w_ret = jnp.concatenate([_rope_perm(wl[:, 0:g]), _rope_perm(wl[:, g:2 * g]), wl[:, 2 * g:4 * g]],
                                axis=1).astype(BF16)
        w_conf = wl[:, 4 * g:6 * g].astype(BF16)
        w_gdn = jnp.concatenate([wl[:, 6 * g:10 * g],
                                 _per_head_to_lanes(wl[:, o_gb:o_gb + N_HEADS]),
                                 _per_head_to_lanes(wl[:, o_gb + N_HEADS:o_gb + 2 * N_HEADS])],
                                axis=1).astype(BF16)
        w_hgrn = wl[:, o_gb + 2 * N_HEADS:].astype(BF16)

        h3 = h.reshape(b, t, d)
        m_ret = _retention(h3, w_ret, cos, sin, row(ret_norm_w[l]))
        cw = jnp.concatenate([conf_conv_w[l], jnp.zeros((CONF_PAD - CONF_KERNEL, g), F32)], axis=0)
        m_conf = _conformer(h3, w_conf, cw, row(conf_conv_b[l]), row(conf_ln_w[l]), row(conf_ln_b[l]))
        gcw = jnp.concatenate([gdn_conv_w[l], jnp.zeros((GDN_PAD - SHORT_CONV, 3 * g), F32)], axis=0)
        m_gdn = _gdn(h3, w_gdn, gcw,
                     row(_per_head_to_lanes(-jnp.exp(gdn_A_log[l].astype(F32)))),
                     row(_per_head_to_lanes(gdn_dt_bias[l].astype(F32))),
                     row(jnp.tile(gdn_norm_w[l], N_HEADS)))
        lb = lb_all[l]
        m_hgrn = _hgrn(h3, w_hgrn, row(jnp.log(lb)), row(jnp.log1p(-lb)), row(1.0 - lb),
                       row(jnp.tile(hgrn_norm_w[l], N_HEADS)))

        last = l == DEPTH - 1
        nxt_w = final_norm_w if last else norm_mix_w[l + 1]
        mixes = [m.reshape(n, g) for m in (m_ret, m_conf, m_gdn, m_hgrn)]
        res = _out_ffn(x2d, mixes, w_out[l].astype(BF16), row(norm_ffn_w[l]),
                       ffn_w_gate[l].astype(BF16), ffn_w_up[l].astype(BF16), ffn_w_down[l].astype(BF16),
                       row(nxt_w), last)
        if last:
            out = res[0]
        else:
            x2d, h = res
    return out.reshape(b, t, d)
```

```python
import functools

import numpy as np
import jax
import jax.numpy as jnp
from jax import lax
from jax.experimental import pallas as pl
from jax.experimental.pallas import tpu as pltpu

F32 = jnp.float32
BF16 = jnp.bfloat16

D_MODEL = 1024
DEPTH = 2
D_GROUP = 256
HEAD_DIM = 64
N_HEADS = 4
CHUNK = 64
SUB = 16
CONF_KERNEL = 31
CONF_PAD = 32
SHORT_CONV = 4
GDN_PAD = 8
ROPE_BASE = 10000.0
D_FF = 2816
FF_CHUNKS = ((0, 1536), (1536, D_FF))
MIX_TBLK = 512
PROJ_CHUNK = 512
FFN_TM = 512
VMEM_LIMIT = 60 * 1024 * 1024

_LOG_GAMMA = [float(np.log1p(-(2.0 ** (-5.0 - h)))) for h in range(N_HEADS)]


def _iota(shape, dim):
    return lax.broadcasted_iota(jnp.int32, shape, dim)


def _dot(a, b):
    return jnp.dot(a, b, preferred_element_type=F32)


def _dot_nt(a, b):
    return lax.dot_general(a, b, (((1,), (1,)), ((), ())), preferred_element_type=F32)


def _dot_tn(a, b):
    return lax.dot_general(a, b, (((0,), (0,)), ((), ())), preferred_element_type=F32)


def _sigmoid(x):
    return 1.0 / (1.0 + jnp.exp(-x))


def _silu(x):
    return x * _sigmoid(x)


def _log1pexp(x):
    return jnp.log(1.0 + jnp.exp(-jnp.abs(x)))


def _softplus(z):
    return jnp.maximum(z, 0.0) + _log1pexp(z)


def _dot_exact_lhs(m, x):
    hi = x.astype(BF16)
    lo = (x - hi.astype(F32)).astype(BF16)
    return _dot(m, hi) + _dot(m, lo)


def _head_sum(x, ones_bd16):
    return _dot(x.astype(BF16), ones_bd16)


def _stack_heads(x16, m16):
    return jnp.concatenate([x16] * N_HEADS, axis=0) * m16


def _select_by_head(hid, vals):
    out = jnp.full(hid.shape, vals[N_HEADS - 1], F32)
    for h in range(N_HEADS - 2, -1, -1):
        out = jnp.where(hid == h, vals[h], out)
    return out


def _interleave(gens):
    done = [0] * len(gens)
    live = list(range(len(gens)))

    def position(j):
        _, n, (start, end) = gens[j]
        return start + (end - start) * (done[j] + 0.5) / n

    while live:
        i = min(live, key=position)
        try:
            next(gens[i][0])
            done[i] += 1
        except StopIteration:
            live.remove(i)


def _ret_stages(p_ref, c0, cos_ref, sin_ref, gnw, o_ref, oc0, s_ref, km, tblk):
    c = CHUNK
    nc = tblk // c
    half = D_GROUP // 2
    lane = _iota((c, D_GROUP), 1)
    row = _iota((c, D_GROUP), 0)
    head_qk = (lane % half) // (HEAD_DIM // 2)
    head_v = lane // HEAD_DIM
    lg_qk = _select_by_head(head_qk, _LOG_GAMMA)
    lg_v = _select_by_head(head_v, _LOG_GAMMA)
    rowf = row.astype(F32)
    diff = row - (lane % c)
    decay_mask = jnp.where(diff >= 0, jnp.exp(jnp.maximum(diff, 0).astype(F32) * lg_v), 0.0)
    q_decay = jnp.exp((rowf + 1.0) * lg_qk)
    k_decay = jnp.exp((c - 1.0 - rowf) * lg_qk) * HEAD_DIM ** -0.5
    chunk_decay = jnp.exp(float(c) * lg_v[0:1, :])

    def rope(off, r0):
        x1 = p_ref[r0:r0 + c, c0 + off:c0 + off + half]
        x2 = p_ref[r0:r0 + c, c0 + off + half:c0 + off + D_GROUP]
        cos = cos_ref[r0:r0 + c, :]
        sin = sin_ref[r0:r0 + c, :]
        return jnp.concatenate([x1 * cos - x2 * sin, x1 * sin + x2 * cos], axis=1)

    rows = [ci * c for ci in range(nc)]
    q = [rope(0, r0) for r0 in rows]
    k = [rope(D_GROUP, r0) for r0 in rows]
    v16 = [p_ref[r0:r0 + c, c0 + 2 * D_GROUP:c0 + 3 * D_GROUP].astype(BF16) for r0 in rows]
    gate = _silu(p_ref[:, c0 + 3 * D_GROUP:c0 + 4 * D_GROUP])
    yield
    kst = [_stack_heads((k[i] * HEAD_DIM ** -0.5).astype(BF16), km.rope_bd16[...]) for i in range(nc)]
    scores = [_dot_nt(q[i].astype(BF16), kst[i]) * decay_mask for i in range(nc)]
    yield
    intra = [_dot(scores[i].astype(BF16), _stack_heads(v16[i], km.bd16[...])) for i in range(nc)]
    kv = [_dot_tn((k[i] * k_decay).astype(BF16), v16[i]) * km.rope_state_f[...] for i in range(nc)]
    qd = [(q[i] * q_decay).astype(BF16) for i in range(nc)]
    yield
    s = s_ref[...]
    outs = []
    for i in range(nc):
        outs.append(intra[i] + _dot(qd[i], s.astype(BF16)))
        s = s * chunk_decay + kv[i]
    s_ref[...] = s
    yield
    o = jnp.concatenate(outs, axis=0)
    mu = _head_sum(o, km.bd16[...]) * (1.0 / HEAD_DIM)
    d = o - mu
    var = _head_sum(d * d, km.bd16[...]) * (1.0 / HEAD_DIM)
    y = d * lax.rsqrt(var + 1e-5) * gnw * gate
    o_ref[0, :, oc0:oc0 + D_GROUP] = y.astype(o_ref.dtype)


_RET_STAGES = 5


def _conf_stages(p_ref, c0, cw_ref, cb, lnw, lnb, o_ref, oc0, buf_ref, sh_ref, tblk):
    buf_ref[CONF_PAD:CONF_PAD + tblk, :] = (p_ref[:, c0:c0 + D_GROUP]
                                            * _sigmoid(p_ref[:, c0 + D_GROUP:c0 + 2 * D_GROUP]))
    yield
    cw = cw_ref[...]
    first = CONF_PAD - (CONF_KERNEL - 1)
    taps_of = [[tap for tap in range(CONF_KERNEL) if (first + tap) % 8 == r] for r in range(8)]
    for r in range(1, 8):
        n_r = first + taps_of[r][-1] - r + tblk
        sh_ref[r - 1, 0:n_r, :] = buf_ref[r:r + n_r, :]
        yield
    acc = None
    for r in range(8):
        for tap in taps_of[r]:
            a0 = first + tap - r
            src = buf_ref[a0:a0 + tblk, :] if r == 0 else sh_ref[r - 1, a0:a0 + tblk, :]
            term = cw[tap:tap + 1, :] * src
            acc = term if acc is None else acc + term
        yield
    buf_ref[0:CONF_PAD, :] = buf_ref[tblk:tblk + CONF_PAD, :]
    y = acc + cb
    mu = jnp.mean(y, axis=-1, keepdims=True)
    d = y - mu
    var = jnp.mean(d * d, axis=-1, keepdims=True)
    z = d * lax.rsqrt(var + 1e-5) * lnw + lnb
    o_ref[0, :, oc0:oc0 + D_GROUP] = _silu(z).astype(o_ref.dtype)


_CONF_STAGES = 17


def _gdn_stages(p_ref, c0, cw_ref, nega, dtb, nw, o_ref, oc0, buf_ref, s_ref, km, tblk):
    buf_ref[GDN_PAD:GDN_PAD + tblk, :] = p_ref[:, c0:c0 + 3 * D_GROUP]
    gate = _silu(p_ref[:, c0 + 3 * D_GROUP:c0 + 4 * D_GROUP])
    beta = _sigmoid(p_ref[:, c0 + 4 * D_GROUP:c0 + 5 * D_GROUP])
    g = nega * _softplus(p_ref[:, c0 + 5 * D_GROUP:c0 + 6 * D_GROUP] + dtb)
    cw = cw_ref[...]
    c = CHUNK
    nc = tblk // c
    grp = D_GROUP
    first = GDN_PAD - (SHORT_CONV - 1)

    lane = _iota((c, D_GROUP), 1)
    row = _iota((c, D_GROUP), 0)
    col_j = lane % c
    eye = row == col_j
    eye_f = jnp.where(eye, 1.0, 0.0)
    incl_f = jnp.where(row >= col_j, 1.0, 0.0)
    strict_f = incl_f - eye_f
    rr = _iota((grp, grp), 0)
    cc = _iota((grp, grp), 1)
    block_tri = jnp.where((rr // c == cc // c) & (rr >= cc), 1.0, 0.0).astype(BF16)

    def bd(x):
        return _stack_heads(x.astype(BF16), km.bd16[...])

    acc = None
    for tap in range(SHORT_CONV):
        term = cw[tap:tap + 1, :] * buf_ref[first + tap:first + tap + tblk, :]
        acc = term if acc is None else acc + term
    buf_ref[0:GDN_PAD, :] = buf_ref[tblk:tblk + GDN_PAD, :]
    qkv = _silu(acc)
    q = qkv[:, 0:D_GROUP]
    k = qkv[:, D_GROUP:2 * D_GROUP]
    v = qkv[:, 2 * D_GROUP:3 * D_GROUP]
    yield
    q = q * lax.rsqrt(_head_sum(q * q, km.bd16[...]) + 1e-6) * HEAD_DIM ** -0.5
    k = k * lax.rsqrt(_head_sum(k * k, km.bd16[...]) + 1e-6)
    yield
    G = jnp.concatenate([_dot_exact_lhs(block_tri, g[i * grp:(i + 1) * grp]) for i in range(tblk // grp)],
                        axis=0)
    eG = jnp.exp(G)
    kb = k * beta
    vb16 = (v * beta).astype(BF16)
    kbe16 = (kb * eG).astype(BF16)
    qe = q * eG
    k16 = k.astype(BF16)
    lhs_aa = jnp.concatenate([kb.reshape(nc, c, D_GROUP), q.reshape(nc, c, D_GROUP)], axis=1).astype(BF16)
    yield
    sl = [slice(i * c, (i + 1) * c) for i in range(nc)]
    Gc = [G[sl[i]] for i in range(nc)]
    g_last = [Gc[i][c - 1:c, :] for i in range(nc)]
    L = []
    for i in range(nc):
        g_row = jnp.sum(eye_f * Gc[i], axis=0, keepdims=True)
        L.append(jnp.exp(jnp.minimum(Gc[i] - g_row, 0.0)))
    yield
    aa = [_dot_nt(lhs_aa[i], bd(k16[sl[i]])) for i in range(nc)]
    A = [aa[i][0:c] * (L[i] * strict_f) for i in range(nc)]
    attn = [(aa[i][c:2 * c] * (L[i] * incl_f)).astype(BF16) for i in range(nc)]
    yield
    T = [eye_f - A[i] for i in range(nc)]
    P = [_dot(A[i].astype(BF16), bd(A[i])) for i in range(nc)]
    yield
    for it in range(5):
        if it < 4:
            tp = [_dot(jnp.concatenate([T[i], P[i]], axis=0).astype(BF16), bd(P[i])) for i in range(nc)]
            T = [T[i] + tp[i][0:c] for i in range(nc)]
            P = [tp[i][c:2 * c] for i in range(nc)]
        else:
            T = [T[i] + _dot(T[i].astype(BF16), bd(P[i])) for i in range(nc)]
        yield
    T16 = [T[i].astype(BF16) for i in range(nc)]
    u = [_dot(T16[i], bd(vb16[sl[i]])) for i in range(nc)]
    w = [_dot(T16[i], bd(kbe16[sl[i]])) for i in range(nc)]
    lhs_ws = [jnp.concatenate([w[i], qe[sl[i]]], axis=0).astype(BF16) for i in range(nc)]
    k_dec = [(k[sl[i]] * jnp.exp(g_last[i] - Gc[i])).astype(BF16) for i in range(nc)]
    e_last = [jnp.exp(g_last[i]) for i in range(nc)]
    yield
    s = s_ref[...]
    outs = []
    for i in range(nc):
        ws = _dot(lhs_ws[i], s.astype(BF16))
        v_new = u[i] - ws[0:c]
        outs.append(ws[c:2 * c] + _dot(attn[i], bd(v_new)))
        s = s * e_last[i] + _dot_tn(k_dec[i], v_new.astype(BF16)) * km.bd_f[...]
        yield
    s_ref[...] = s
    o = jnp.concatenate(outs, axis=0)
    ms = _head_sum(o * o, km.bd16[...]) * (1.0 / HEAD_DIM)
    y = o * lax.rsqrt(ms + 1e-6) * nw * gate
    o_ref[0, :, oc0:oc0 + D_GROUP] = y.astype(o_ref.dtype)


_GDN_STAGES = 21


def _hgrn_stages(p_ref, c0, log_lb, l1m, oml, nw, o_ref, oc0, st_ref, km, tblk):
    q_all = p_ref[:, c0:c0 + D_GROUP]
    x_all = p_ref[:, c0 + D_GROUP:c0 + 2 * D_GROUP]
    v_all = p_ref[:, c0 + 2 * D_GROUP:c0 + 3 * D_GROUP]
    gate = _silu(p_ref[:, c0 + 3 * D_GROUP:c0 + 4 * D_GROUP])
    grp = D_GROUP
    nsub = grp // SUB
    hsub = SUB // 2
    rr = _iota((grp, grp), 0)
    cc = _iota((grp, grp), 1)
    block_tri = jnp.where((rr // SUB == cc // SUB) & (rr >= cc), 1.0, 0.0).astype(BF16)
    row_in_sub = _iota((nsub, SUB, D_GROUP), 1)
    row_in_half = _iota((nsub, hsub, D_GROUP), 1) + hsub

    groups, qg_all, kv_all, e_end_all = [], [], [], []
    for gi in range(tblk // grp):
        r0 = gi * grp
        q = q_all[r0:r0 + grp]
        x = x_all[r0:r0 + grp]
        v = v_all[r0:r0 + grp]
        b_ = l1m + (jnp.minimum(x, 0.0) - _log1pexp(x))
        log_f = jnp.maximum(log_lb, b_) + _log1pexp(log_lb - b_)
        k = oml * _sigmoid(-x)
        G = _dot_exact_lhs(block_tri, log_f)
        G3 = G.reshape(nsub, SUB, D_GROUP)
        k3 = k.reshape(nsub, SUB, D_GROUP)
        groups.append((G3, q.reshape(nsub, SUB, D_GROUP), k3, v.reshape(nsub, SUB, D_GROUP)))
        eG = jnp.exp(G)
        qg = (q * eG).astype(BF16)
        k_end = (k3 * jnp.exp(G3[:, SUB - 1:SUB, :] - G3)).reshape(grp, D_GROUP).astype(BF16)
        v16 = v.astype(BF16)
        for bi in range(nsub):
            lo = bi * SUB
            qg_all.append(qg[lo:lo + SUB])
            e_end_all.append(eG[lo + SUB - 1:lo + SUB, :])
            kv_all.append(_dot_tn(v16[lo:lo + SUB], k_end[lo:lo + SUB]) * km.bd_f[...])
        yield

    st = st_ref[...]
    inter = []
    for bi in range(len(kv_all)):
        inter.append(_dot_nt(qg_all[bi], st.astype(BF16)))
        st = st * e_end_all[bi] + kv_all[bi]
        if bi % 4 == 3:
            yield
    st_ref[...] = st

    intra = []
    for G3, q3, k3, v3 in groups:
        full = None
        for j in range(hsub):
            e = jnp.exp(jnp.where(row_in_sub >= j, G3 - G3[:, j:j + 1, :], -1e30))
            z = q3 * e * k3[:, j:j + 1, :]
            sc = _head_sum(z.reshape(grp, D_GROUP), km.bd16[...])
            term = sc.reshape(nsub, SUB, D_GROUP) * v3[:, j:j + 1, :]
            full = term if full is None else full + term
            yield
        Gh = G3[:, hsub:SUB, :]
        qh = q3[:, hsub:SUB, :]
        upper = None
        for j in range(hsub, SUB):
            e = jnp.exp(jnp.where(row_in_half >= j, Gh - G3[:, j:j + 1, :], -1e30))
            z = qh * e * k3[:, j:j + 1, :]
            sc = _head_sum(z.reshape(grp // 2, D_GROUP), km.bd16[...])
            term = sc.reshape(nsub, hsub, D_GROUP) * v3[:, j:j + 1, :]
            upper = term if upper is None else upper + term
            yield
        intra.append(jnp.concatenate([full[:, 0:hsub, :], full[:, hsub:SUB, :] + upper],
                                     axis=1).reshape(grp, D_GROUP))
    o = jnp.concatenate(intra, axis=0) + jnp.concatenate(inter, axis=0)
    ms = _head_sum(o * o, km.bd16[...]) * (1.0 / HEAD_DIM)
    y = o * lax.rsqrt(ms + 1e-6) * nw * gate
    o_ref[0, :, oc0:oc0 + D_GROUP] = y.astype(o_ref.dtype)


_HGRN_STAGES = 43


_C_RET, _C_CONF, _C_GDN, _C_HGRN, _C_END = 0, 4 * D_GROUP, 6 * D_GROUP, 12 * D_GROUP, 16 * D_GROUP
_VEC_ROWS = 16


class _HeadMasks:
    def __init__(self, bd16, bd_f, rope_bd16, rope_state_f):
        self.bd16 = bd16
        self.bd_f = bd_f
        self.rope_bd16 = rope_bd16
        self.rope_state_f = rope_state_f


def _project(x_ref, nmw_ref, w_ref, p_ref):
    x = x_ref[0]
    h = (x * lax.rsqrt(jnp.mean(x * x, axis=-1, keepdims=True) + 1e-6) * nmw_ref[...]).astype(BF16)
    for c0 in range(0, _C_END, PROJ_CHUNK):
        p_ref[:, c0:c0 + PROJ_CHUNK] = _dot(h, w_ref[:, c0:c0 + PROJ_CHUNK])


def _mix_kernel(x_ref, nmw_ref, w_ref, cos_ref, sin_ref, ccw_ref, gcw_ref, vec_ref, o_ref,
                p_ref, cbuf_ref, csh_ref, gbuf_ref, sret_ref, sgdn_ref, shg_ref,
                bd16_ref, bdf_ref, rbd16_ref, rsf_ref, *, tblk):
    @pl.when(pl.program_id(1) == 0)
    def _():
        r = _iota((D_GROUP, D_GROUP), 0)
        l = _iota((D_GROUP, D_GROUP), 1)
        rope_head = lambda i: (i % (D_GROUP // 2)) // (HEAD_DIM // 2)
        bd = jnp.where(r // HEAD_DIM == l // HEAD_DIM, 1.0, 0.0)
        bdf_ref[...] = bd
        bd16_ref[...] = bd.astype(BF16)
        rbd16_ref[...] = jnp.where(r // HEAD_DIM == rope_head(l), 1.0, 0.0).astype(BF16)
        rsf_ref[...] = jnp.where(rope_head(r) == l // HEAD_DIM, 1.0, 0.0)

    @pl.when(pl.program_id(1) == 0)
    def _():
        sret_ref[...] = jnp.zeros_like(sret_ref)
        sgdn_ref[...] = jnp.zeros_like(sgdn_ref)
        shg_ref[...] = jnp.zeros_like(shg_ref)
        cbuf_ref[0:CONF_PAD, :] = jnp.zeros((CONF_PAD, D_GROUP), F32)
        gbuf_ref[0:GDN_PAD, :] = jnp.zeros((GDN_PAD, 3 * D_GROUP), F32)

    km = _HeadMasks(bd16_ref, bdf_ref, rbd16_ref, rsf_ref)
    _project(x_ref, nmw_ref, w_ref, p_ref)
    vec = lambda i: vec_ref[i:i + 1, :]
    g = D_GROUP
    _interleave([
        (_ret_stages(p_ref, _C_RET, cos_ref, sin_ref, vec(0), o_ref, 0, sret_ref, km, tblk),
         _RET_STAGES, (0.0, 0.4)),
        (_conf_stages(p_ref, _C_CONF, ccw_ref, vec(1), vec(2), vec(3), o_ref, g,
                      cbuf_ref, csh_ref, tblk), _CONF_STAGES, (0.1, 0.9)),
        (_gdn_stages(p_ref, _C_GDN, gcw_ref, vec(4), vec(5), vec(6), o_ref, 2 * g,
                     gbuf_ref, sgdn_ref, km, tblk), _GDN_STAGES, (0.0, 0.85)),
        (_hgrn_stages(p_ref, _C_HGRN, vec(7), vec(8), vec(9), vec(10), o_ref, 3 * g,
                      shg_ref, km, tblk), _HGRN_STAGES, (0.0, 1.0)),
    ])


def _mixers(x, nmw, w, cos, sin, ccw, gcw, vec):
    b, t, _ = x.shape
    tblk = min(MIX_TBLK, t)
    assert t % tblk == 0
    nt = t // tblk
    const = lambda a: pl.BlockSpec(a.shape, lambda bi, ti: (0, 0), pipeline_mode=pl.Buffered(1))
    timed = lambda a: pl.BlockSpec((tblk, a.shape[1]), lambda bi, ti: (ti, 0))
    blk = pl.BlockSpec((1, tblk, D_MODEL), lambda bi, ti: (bi, ti, 0))

    return pl.pallas_call(
        functools.partial(_mix_kernel, tblk=tblk),
        name="mixers",
        grid=(b, nt),
        in_specs=[blk, const(nmw), const(w), timed(cos), timed(sin), const(ccw), const(gcw), const(vec)],
        out_specs=blk,
        out_shape=jax.ShapeDtypeStruct((b, t, D_MODEL), BF16),
        scratch_shapes=[
            pltpu.VMEM((tblk, _C_END), F32),
            pltpu.VMEM((tblk + CONF_PAD, D_GROUP), F32),
            pltpu.VMEM((7, tblk + CONF_PAD, D_GROUP), F32),
            pltpu.VMEM((tblk + GDN_PAD, 3 * D_GROUP), F32),
            pltpu.VMEM((D_GROUP, D_GROUP), F32),
            pltpu.VMEM((D_GROUP, D_GROUP), F32),
            pltpu.VMEM((D_GROUP, D_GROUP), F32),
            pltpu.VMEM((D_GROUP, D_GROUP), BF16),
            pltpu.VMEM((D_GROUP, D_GROUP), F32),
            pltpu.VMEM((D_GROUP, D_GROUP), BF16),
            pltpu.VMEM((D_GROUP, D_GROUP), F32),
        ],
        compiler_params=pltpu.CompilerParams(
            dimension_semantics=("parallel", "arbitrary"), vmem_limit_bytes=VMEM_LIMIT),
    )(x, nmw, w, cos, sin, ccw, gcw, vec)


def _ffn_kernel(x_ref, m_ref, wo_ref, nfw_ref, wg_ref, wu_ref, wd_ref, nxt_ref, o_ref, *, last):
    y = x_ref[...] + _dot(m_ref[...], wo_ref[...])
    h = (y * lax.rsqrt(jnp.mean(y * y, axis=-1, keepdims=True) + 1e-6) * nfw_ref[...]).astype(BF16)
    down = None
    for f0, f1 in FF_CHUNKS:
        gate = _dot(h, wg_ref[:, f0:f1])
        up = _dot(h, wu_ref[:, f0:f1])
        part = _dot((_silu(gate) * up).astype(BF16), wd_ref[f0:f1, :])
        down = part if down is None else down + part
    y = y + down
    if last:
        y = y * lax.rsqrt(jnp.mean(y * y, axis=-1, keepdims=True) + 1e-6) * nxt_ref[...]
    o_ref[...] = y


def _out_ffn(x2d, mix, wo, nfw, wg, wu, wd, nxt_w, last):
    n = x2d.shape[0]
    tm = min(FFN_TM, n)
    row = pl.BlockSpec((tm, D_MODEL), lambda i: (i, 0))
    const = lambda shape: pl.BlockSpec(shape, lambda i: (0, 0), pipeline_mode=pl.Buffered(1))
    return pl.pallas_call(
        functools.partial(_ffn_kernel, last=last),
        name="out_ffn_last" if last else "out_ffn",
        grid=(n // tm,),
        in_specs=[row, row, const((D_MODEL, D_MODEL)), const((1, D_MODEL)), const((D_MODEL, D_FF)),
                  const((D_MODEL, D_FF)), const((D_FF, D_MODEL)), const((1, D_MODEL))],
        out_specs=row,
        out_shape=jax.ShapeDtypeStruct((n, D_MODEL), F32),
        compiler_params=pltpu.CompilerParams(
            dimension_semantics=("parallel",), vmem_limit_bytes=VMEM_LIMIT),
    )(x2d, mix, wo, nfw, wg, wu, wd, nxt_w)


def _rope_perm(w):
    d = w.shape[0]
    return w.reshape(d, N_HEADS, 2, HEAD_DIM // 2).transpose(0, 2, 1, 3).reshape(d, D_GROUP)


def _per_head_to_lanes(a):
    return jnp.repeat(a, HEAD_DIM, axis=-1)


def kernel(x, norm_mix_w, w_in, ret_norm_w, conf_conv_w, conf_conv_b, conf_ln_w, conf_ln_b, gdn_conv_w, gdn_A_log, gdn_dt_bias, gdn_norm_w, hgrn_lb_logits, hgrn_norm_w, w_out, norm_ffn_w, ffn_w_gate, ffn_w_up, ffn_w_down, final_norm_w):
    b, t, d = x.shape
    n = b * t
    g = D_GROUP
    row = lambda a: a.reshape(1, -1).astype(F32)

    pos = jnp.arange(t, dtype=F32)
    half = HEAD_DIM // 2
    inv = ROPE_BASE ** (-jnp.arange(half, dtype=F32) / half)
    ang = pos[:, None] * inv[None, :]
    cos = jnp.tile(jnp.cos(ang), (1, N_HEADS))
    sin = jnp.tile(jnp.sin(ang), (1, N_HEADS))

    lb_all = jnp.cumsum(jax.nn.softmax(hgrn_lb_logits.astype(F32), axis=0), axis=0)
    lb_all = lb_all - lb_all[0:1]

    for l in range(DEPTH):
        wl = w_in[l]
        o_gb = 10 * g
        w_all = jnp.concatenate([
            _rope_perm(wl[:, 0:g]), _rope_perm(wl[:, g:2 * g]), wl[:, 2 * g:4 * g],
            wl[:, 4 * g:6 * g],
            wl[:, 6 * g:10 * g], _per_head_to_lanes(wl[:, o_gb:o_gb + N_HEADS]),
            _per_head_to_lanes(wl[:, o_gb + N_HEADS:o_gb + 2 * N_HEADS]),
            wl[:, o_gb + 2 * N_HEADS:],
        ], axis=1).astype(BF16)
        lb = lb_all[l]
        vec = jnp.concatenate([
            row(ret_norm_w[l]), row(conf_conv_b[l]), row(conf_ln_w[l]), row(conf_ln_b[l]),
            row(_per_head_to_lanes(-jnp.exp(gdn_A_log[l].astype(F32)))),
            row(_per_head_to_lanes(gdn_dt_bias[l].astype(F32))),
            row(jnp.tile(gdn_norm_w[l], N_HEADS)),
            row(jnp.log(lb)), row(jnp.log1p(-lb)), row(1.0 - lb), row(jnp.tile(hgrn_norm_w[l], N_HEADS)),
            jnp.zeros((_VEC_ROWS - 11, g), F32)], axis=0)
        ccw = jnp.concatenate([conf_conv_w[l].astype(F32), jnp.zeros((CONF_PAD - CONF_KERNEL, g), F32)], axis=0)
        gcw = jnp.concatenate([gdn_conv_w[l].astype(F32), jnp.zeros((GDN_PAD - SHORT_CONV, 3 * g), F32)], axis=0)
        mix = _mixers(x, row(norm_mix_w[l]), w_all, cos, sin, ccw, gcw, vec)

        last = l == DEPTH - 1
        x = _out_ffn(x.reshape(n, d), mix.reshape(n, d), w_out[l].astype(BF16), row(norm_ffn_w[l]),
                     ffn_w_gate[l].astype(BF16), ffn_w_up[l].astype(BF16), ffn_w_down[l].astype(BF16),
                     row(final_norm_w), last).reshape(b, t, d)
    return x
```

```python
import functools

import numpy as np
import jax
import jax.numpy as jnp
from jax import lax
from jax.experimental import pallas as pl
from jax.experimental.pallas import tpu as pltpu

F32 = jnp.float32
BF16 = jnp.bfloat16

D_MODEL = 1024
DEPTH = 2
D_GROUP = 256
HEAD_DIM = 64
N_HEADS = 4
CHUNK = 64
SUB = 16
CONF_KERNEL = 31
CONF_PAD = 32
SHORT_CONV = 4
GDN_PAD = 8
ROPE_BASE = 10000.0
D_FF = 2816
FF_CHUNKS = ((0, 1536), (1536, D_FF))
MIX_TBLK = 512
ROW_GROUP = 256
PROJ_CHUNK = 512
FFN_TM = 1024
VMEM_LIMIT = 60 * 1024 * 1024
LOG2E = 1.4426950408889634

_LOG_GAMMA = [float(np.log1p(-(2.0 ** (-5.0 - h)))) for h in range(N_HEADS)]


def _iota(shape, dim):
    return lax.broadcasted_iota(jnp.int32, shape, dim)


def _dot(a, b):
    return jnp.dot(a, b, preferred_element_type=F32)


def _dot_nt(a, b):
    return lax.dot_general(a, b, (((1,), (1,)), ((), ())), preferred_element_type=F32)


def _dot_tn(a, b):
    return lax.dot_general(a, b, (((0,), (0,)), ((), ())), preferred_element_type=F32)


def _sigmoid(x):
    return 1.0 / (1.0 + jnp.exp2(x * -LOG2E))


def _silu(x):
    return x * _sigmoid(x)


def _log1pexp(x):
    return jnp.log(1.0 + jnp.exp2(jnp.abs(x) * -LOG2E))


def _softplus(z):
    return jnp.maximum(z, 0.0) + _log1pexp(z)


def _dot_exact_lhs(m, x):
    hi = x.astype(BF16)
    lo = (x - hi.astype(F32)).astype(BF16)
    return _dot(m, hi) + _dot(m, lo)


def _head_sum(x, ones_bd16):
    return _dot(x.astype(BF16), ones_bd16)


def _stack_heads(x16, m16):
    return jnp.concatenate([x16] * N_HEADS, axis=0) * m16


def _select_by_head(hid, vals):
    out = jnp.full(hid.shape, vals[N_HEADS - 1], F32)
    for h in range(N_HEADS - 2, -1, -1):
        out = jnp.where(hid == h, vals[h], out)
    return out


def _interleave(gens):
    done = [0] * len(gens)
    live = list(range(len(gens)))
    while live:
        i = min(live, key=lambda j: (done[j] + 0.5) / gens[j][1])
        try:
            next(gens[i][0])
            done[i] += 1
        except StopIteration:
            live.remove(i)


def _ret_stages(p_ref, c0, cos_ref, sin_ref, gnw, o_ref, oc0, s_ref, km, tblk):
    c = CHUNK
    nc = tblk // c
    half = D_GROUP // 2
    lane = _iota((c, D_GROUP), 1)
    row = _iota((c, D_GROUP), 0)
    head_qk = (lane % half) // (HEAD_DIM // 2)
    head_v = lane // HEAD_DIM
    lg_qk = _select_by_head(head_qk, _LOG_GAMMA)
    lg_v = _select_by_head(head_v, _LOG_GAMMA)
    rowf = row.astype(F32)
    diff = row - (lane % c)
    decay_mask = jnp.where(diff >= 0, jnp.exp(jnp.maximum(diff, 0).astype(F32) * lg_v), 0.0)
    q_decay = jnp.exp((rowf + 1.0) * lg_qk)
    k_decay = jnp.exp((c - 1.0 - rowf) * lg_qk) * HEAD_DIM ** -0.5
    chunk_decay = jnp.exp(float(c) * lg_v[0:1, :])

    def rope(off, r0):
        x1 = p_ref[r0:r0 + c, c0 + off:c0 + off + half]
        x2 = p_ref[r0:r0 + c, c0 + off + half:c0 + off + D_GROUP]
        cos = cos_ref[r0:r0 + c, :]
        sin = sin_ref[r0:r0 + c, :]
        return jnp.concatenate([x1 * cos - x2 * sin, x1 * sin + x2 * cos], axis=1)

    rows = [ci * c for ci in range(nc)]
    q = [rope(0, r0) for r0 in rows]
    k = [rope(D_GROUP, r0) for r0 in rows]
    v16 = [p_ref[r0:r0 + c, c0 + 2 * D_GROUP:c0 + 3 * D_GROUP].astype(BF16) for r0 in rows]
    yield
    kst = [_stack_heads((k[i] * HEAD_DIM ** -0.5).astype(BF16), km.rope_bd16[...]) for i in range(nc)]
    scores = [_dot_nt(q[i].astype(BF16), kst[i]) * decay_mask for i in range(nc)]
    yield
    intra = [_dot(scores[i].astype(BF16), _stack_heads(v16[i], km.bd16[...])) for i in range(nc)]
    kv = [_dot_tn((k[i] * k_decay).astype(BF16), v16[i]) * km.rope_state_f[...] for i in range(nc)]
    qd = [(q[i] * q_decay).astype(BF16) for i in range(nc)]
    yield
    s = s_ref[...]
    outs = []
    for i in range(nc):
        outs.append(intra[i] + _dot(qd[i], s.astype(BF16)))
        s = s * chunk_decay + kv[i]
    s_ref[...] = s
    yield
    o = jnp.concatenate(outs, axis=0)
    mu = _head_sum(o, km.bd16[...]) * (1.0 / HEAD_DIM)
    d = o - mu
    var = _head_sum(d * d, km.bd16[...]) * (1.0 / HEAD_DIM)
    y = d * lax.rsqrt(var + 1e-5) * gnw * _silu(p_ref[:, c0 + 3 * D_GROUP:c0 + 4 * D_GROUP])
    o_ref[0, :, oc0:oc0 + D_GROUP] = y.astype(o_ref.dtype)


_RET_STAGES = 5


def _conf_stages(p_ref, c0, cw_ref, cb, lnw, lnb, o_ref, oc0, buf_ref, sh_ref, tblk):
    cw = cw_ref[...]
    grp = ROW_GROUP
    first = CONF_PAD - (CONF_KERNEL - 1)
    taps_of = [[tap for tap in range(CONF_KERNEL) if (first + tap) % 8 == r] for r in range(8)]
    for r0 in range(0, tblk, grp):
        buf_ref[CONF_PAD + r0:CONF_PAD + r0 + grp, :] = (
            p_ref[r0:r0 + grp, c0:c0 + D_GROUP] * _sigmoid(p_ref[r0:r0 + grp, c0 + D_GROUP:c0 + 2 * D_GROUP]))
        yield
        for r in range(1, 8):
            n_r = first + taps_of[r][-1] - r + grp
            sh_ref[r - 1, r0:r0 + n_r, :] = buf_ref[r0 + r:r0 + r + n_r, :]
        yield
        acc = None
        for r in range(8):
            for tap in taps_of[r]:
                a0 = r0 + first + tap - r
                src = buf_ref[a0:a0 + grp, :] if r == 0 else sh_ref[r - 1, a0:a0 + grp, :]
                term = cw[tap:tap + 1, :] * src
                acc = term if acc is None else acc + term
            if r % 2 == 1:
                yield
        y = acc + cb
        mu = jnp.mean(y, axis=-1, keepdims=True)
        d = y - mu
        var = jnp.mean(d * d, axis=-1, keepdims=True)
        z = d * lax.rsqrt(var + 1e-5) * lnw + lnb
        o_ref[0, r0:r0 + grp, oc0:oc0 + D_GROUP] = _silu(z).astype(o_ref.dtype)
        yield
    buf_ref[0:CONF_PAD, :] = buf_ref[tblk:tblk + CONF_PAD, :]


_CONF_STAGES = 15


def _gdn_stages(p_ref, c0, cw_ref, nega, dtb, nw, o_ref, oc0, buf_ref, s_ref, km, tblk):
    cw = cw_ref[...]
    c = CHUNK
    nc = tblk // c
    grp = ROW_GROUP
    first = GDN_PAD - (SHORT_CONV - 1)

    lane = _iota((c, D_GROUP), 1)
    row = _iota((c, D_GROUP), 0)
    col_j = lane % c
    eye_f = jnp.where(row == col_j, 1.0, 0.0)
    incl_f = jnp.where(row >= col_j, 1.0, 0.0)
    strict_f = incl_f - eye_f
    rr = _iota((grp, grp), 0)
    cc = _iota((grp, grp), 1)
    block_tri = jnp.where((rr // c == cc // c) & (rr >= cc), 1.0, 0.0).astype(BF16)

    def bd(x):
        return _stack_heads(x.astype(BF16), km.bd16[...])

    Gc, kc, k16, vb16, kbe16, qe, lhs_aa = [], [], [], [], [], [], []
    for r0 in range(0, tblk, grp):
        buf_ref[GDN_PAD + r0:GDN_PAD + r0 + grp, :] = p_ref[r0:r0 + grp, c0:c0 + 3 * D_GROUP]
        beta = _sigmoid(p_ref[r0:r0 + grp, c0 + 4 * D_GROUP:c0 + 5 * D_GROUP])
        g = nega * _softplus(p_ref[r0:r0 + grp, c0 + 5 * D_GROUP:c0 + 6 * D_GROUP] + dtb)
        acc = None
        for tap in range(SHORT_CONV):
            term = cw[tap:tap + 1, :] * buf_ref[r0 + first + tap:r0 + first + tap + grp, :]
            acc = term if acc is None else acc + term
        qkv = _silu(acc)
        q = qkv[:, 0:D_GROUP]
        k = qkv[:, D_GROUP:2 * D_GROUP]
        v = qkv[:, 2 * D_GROUP:3 * D_GROUP]
        yield
        q = q * lax.rsqrt(_head_sum(q * q, km.bd16[...]) + 1e-6) * HEAD_DIM ** -0.5
        k = k * lax.rsqrt(_head_sum(k * k, km.bd16[...]) + 1e-6)
        G = _dot_exact_lhs(block_tri, g)
        eG = jnp.exp(G)
        kb = k * beta
        vb = (v * beta).astype(BF16)
        kbe = (kb * eG).astype(BF16)
        qeg = q * eG
        kg16 = k.astype(BF16)
        for j in range(0, grp, c):
            Gc.append(G[j:j + c])
            kc.append(k[j:j + c])
            k16.append(kg16[j:j + c])
            vb16.append(vb[j:j + c])
            kbe16.append(kbe[j:j + c])
            qe.append(qeg[j:j + c])
            lhs_aa.append(jnp.concatenate([kb[j:j + c], q[j:j + c]], axis=0).astype(BF16))
        yield
    buf_ref[0:GDN_PAD, :] = buf_ref[tblk:tblk + GDN_PAD, :]

    g_last = [Gc[i][c - 1:c, :] for i in range(nc)]
    L = []
    for i in range(nc):
        g_row = jnp.sum(eye_f * Gc[i], axis=0, keepdims=True)
        L.append(jnp.exp(jnp.minimum(Gc[i] - g_row, 0.0)))
    yield
    aa = [_dot_nt(lhs_aa[i], bd(k16[i])) for i in range(nc)]
    A = [aa[i][0:c] * (L[i] * strict_f) for i in range(nc)]
    attn = [(aa[i][c:2 * c] * (L[i] * incl_f)).astype(BF16) for i in range(nc)]
    yield
    T = [eye_f - A[i] for i in range(nc)]
    P = [_dot(A[i].astype(BF16), bd(A[i])) for i in range(nc)]
    yield
    for it in range(5):
        if it < 4:
            tp = [_dot(jnp.concatenate([T[i], P[i]], axis=0).astype(BF16), bd(P[i])) for i in range(nc)]
            T = [T[i] + tp[i][0:c] for i in range(nc)]
            P = [tp[i][c:2 * c] for i in range(nc)]
        else:
            T = [T[i] + _dot(T[i].astype(BF16), bd(P[i])) for i in range(nc)]
        yield
    T16 = [T[i].astype(BF16) for i in range(nc)]
    u = [_dot(T16[i], bd(vb16[i])) for i in range(nc)]
    w = [_dot(T16[i], bd(kbe16[i])) for i in range(nc)]
    lhs_ws = [jnp.concatenate([w[i], qe[i]], axis=0).astype(BF16) for i in range(nc)]
    k_dec = [(kc[i] * jnp.exp(g_last[i] - Gc[i])).astype(BF16) for i in range(nc)]
    e_last = [jnp.exp(g_last[i]) for i in range(nc)]
    yield
    s = s_ref[...]
    outs = []
    for i in range(nc):
        ws = _dot(lhs_ws[i], s.astype(BF16))
        v_new = u[i] - ws[0:c]
        outs.append(ws[c:2 * c] + _dot(attn[i], bd(v_new)))
        s = s * e_last[i] + _dot_tn(k_dec[i], v_new.astype(BF16)) * km.bd_f[...]
        yield
    s_ref[...] = s
    o = jnp.concatenate(outs, axis=0)
    ms = _head_sum(o * o, km.bd16[...]) * (1.0 / HEAD_DIM)
    y = o * lax.rsqrt(ms + 1e-6) * nw * _silu(p_ref[:, c0 + 3 * D_GROUP:c0 + 4 * D_GROUP])
    o_ref[0, :, oc0:oc0 + D_GROUP] = y.astype(o_ref.dtype)


_GDN_STAGES = 22


def _hgrn_stages(p_ref, c0, log_lb, l1m, oml, nw, o_ref, oc0, st_ref, km, tblk):
    grp = ROW_GROUP
    nsub = grp // SUB
    hsub = SUB // 2
    rr = _iota((grp, grp), 0)
    cc = _iota((grp, grp), 1)
    block_tri = jnp.where((rr // SUB == cc // SUB) & (rr >= cc), 1.0, 0.0).astype(BF16)
    row_in_sub = _iota((nsub, SUB, D_GROUP), 1)
    row_in_half = _iota((nsub, hsub, D_GROUP), 1) + hsub

    groups, qg_all, kv_all, e_end_all = [], [], [], []
    for r0 in range(0, tblk, grp):
        q = p_ref[r0:r0 + grp, c0:c0 + D_GROUP]
        x = p_ref[r0:r0 + grp, c0 + D_GROUP:c0 + 2 * D_GROUP]
        v = p_ref[r0:r0 + grp, c0 + 2 * D_GROUP:c0 + 3 * D_GROUP]
        b_ = l1m + (jnp.minimum(x, 0.0) - _log1pexp(x))
        log_f = jnp.maximum(log_lb, b_) + _log1pexp(log_lb - b_)
        k = oml * _sigmoid(-x)
        G = _dot_exact_lhs(block_tri, log_f)
        G3 = G.reshape(nsub, SUB, D_GROUP)
        k3 = k.reshape(nsub, SUB, D_GROUP)
        groups.append((G3 * LOG2E, q.reshape(nsub, SUB, D_GROUP), k3, v.reshape(nsub, SUB, D_GROUP)))
        eG = jnp.exp(G)
        qg = (q * eG).astype(BF16)
        k_end = (k3 * jnp.exp(G3[:, SUB - 1:SUB, :] - G3)).reshape(grp, D_GROUP).astype(BF16)
        v16 = v.astype(BF16)
        for bi in range(nsub):
            lo = bi * SUB
            qg_all.append(qg[lo:lo + SUB])
            e_end_all.append(eG[lo + SUB - 1:lo + SUB, :])
            kv_all.append(_dot_tn(v16[lo:lo + SUB], k_end[lo:lo + SUB]) * km.bd_f[...])
        yield

    st = st_ref[...]
    inter = []
    for bi in range(len(kv_all)):
        inter.append(_dot_nt(qg_all[bi], st.astype(BF16)))
        st = st * e_end_all[bi] + kv_all[bi]
        if bi % 4 == 3:
            yield
    st_ref[...] = st

    intra = []
    for G2, q3, k3, v3 in groups:
        full = None
        for j in range(hsub):
            e = jnp.exp2(jnp.where(row_in_sub >= j, G2 - G2[:, j:j + 1, :], -1e30))
            z = q3 * e * k3[:, j:j + 1, :]
            sc = _head_sum(z.reshape(grp, D_GROUP), km.bd16[...])
            term = sc.reshape(nsub, SUB, D_GROUP) * v3[:, j:j + 1, :]
            full = term if full is None else full + term
            yield
        Gh = G2[:, hsub:SUB, :]
        qh = q3[:, hsub:SUB, :]
        upper = None
        for j in range(hsub, SUB):
            e = jnp.exp2(jnp.where(row_in_half >= j, Gh - G2[:, j:j + 1, :], -1e30))
            z = qh * e * k3[:, j:j + 1, :]
            sc = _head_sum(z.reshape(grp // 2, D_GROUP), km.bd16[...])
            term = sc.reshape(nsub, hsub, D_GROUP) * v3[:, j:j + 1, :]
            upper = term if upper is None else upper + term
            yield
        intra.append(jnp.concatenate([full[:, 0:hsub, :], full[:, hsub:SUB, :] + upper],
                                     axis=1).reshape(grp, D_GROUP))
    o = jnp.concatenate(intra, axis=0) + jnp.concatenate(inter, axis=0)
    ms = _head_sum(o * o, km.bd16[...]) * (1.0 / HEAD_DIM)
    y = o * lax.rsqrt(ms + 1e-6) * nw * _silu(p_ref[:, c0 + 3 * D_GROUP:c0 + 4 * D_GROUP])
    o_ref[0, :, oc0:oc0 + D_GROUP] = y.astype(o_ref.dtype)


_HGRN_STAGES = 43


_C_RET, _C_CONF, _C_GDN, _C_HGRN, _C_END = 0, 4 * D_GROUP, 6 * D_GROUP, 12 * D_GROUP, 16 * D_GROUP
_VEC_ROWS = 16


class _HeadMasks:
    def __init__(self, bd16, bd_f, rope_bd16, rope_state_f):
        self.bd16 = bd16
        self.bd_f = bd_f
        self.rope_bd16 = rope_bd16
        self.rope_state_f = rope_state_f


def _project(x_ref, nmw_ref, w_ref, p_ref, tblk):
    for r0 in range(0, tblk, ROW_GROUP):
        x = x_ref[0, r0:r0 + ROW_GROUP, :]
        h = (x * lax.rsqrt(jnp.mean(x * x, axis=-1, keepdims=True) + 1e-6) * nmw_ref[...]).astype(BF16)
        for c0 in range(0, _C_END, PROJ_CHUNK):
            p_ref[r0:r0 + ROW_GROUP, c0:c0 + PROJ_CHUNK] = _dot(h, w_ref[:, c0:c0 + PROJ_CHUNK])


def _mix_kernel(x_ref, nmw_ref, w_ref, cos_ref, sin_ref, ccw_ref, gcw_ref, vec_ref, o_ref,
                p_ref, cbuf_ref, csh_ref, gbuf_ref, sret_ref, sgdn_ref, shg_ref,
                bd16_ref, bdf_ref, rbd16_ref, rsf_ref, *, tblk):
    @pl.when(pl.program_id(1) == 0)
    def _():
        r = _iota((D_GROUP, D_GROUP), 0)
        l = _iota((D_GROUP, D_GROUP), 1)
        rope_head = lambda i: (i % (D_GROUP // 2)) // (HEAD_DIM // 2)
        bd = jnp.where(r // HEAD_DIM == l // HEAD_DIM, 1.0, 0.0)
        bdf_ref[...] = bd
        bd16_ref[...] = bd.astype(BF16)
        rbd16_ref[...] = jnp.where(r // HEAD_DIM == rope_head(l), 1.0, 0.0).astype(BF16)
        rsf_ref[...] = jnp.where(rope_head(r) == l // HEAD_DIM, 1.0, 0.0)
        sret_ref[...] = jnp.zeros_like(sret_ref)
        sgdn_ref[...] = jnp.zeros_like(sgdn_ref)
        shg_ref[...] = jnp.zeros_like(shg_ref)
        cbuf_ref[0:CONF_PAD, :] = jnp.zeros((CONF_PAD, D_GROUP), F32)
        gbuf_ref[0:GDN_PAD, :] = jnp.zeros((GDN_PAD, 3 * D_GROUP), F32)

    km = _HeadMasks(bd16_ref, bdf_ref, rbd16_ref, rsf_ref)
    _project(x_ref, nmw_ref, w_ref, p_ref, tblk)
    vec = lambda i: vec_ref[i:i + 1, :]
    g = D_GROUP
    _interleave([
        (_ret_stages(p_ref, _C_RET, cos_ref, sin_ref, vec(0), o_ref, 0, sret_ref, km, tblk), _RET_STAGES),
        (_conf_stages(p_ref, _C_CONF, ccw_ref, vec(1), vec(2), vec(3), o_ref, g, cbuf_ref, csh_ref, tblk),
         _CONF_STAGES),
        (_gdn_stages(p_ref, _C_GDN, gcw_ref, vec(4), vec(5), vec(6), o_ref, 2 * g, gbuf_ref, sgdn_ref, km, tblk),
         _GDN_STAGES),
        (_hgrn_stages(p_ref, _C_HGRN, vec(7), vec(8), vec(9), vec(10), o_ref, 3 * g, shg_ref, km, tblk),
         _HGRN_STAGES),
    ])


def _mixers(x, nmw, w, cos, sin, ccw, gcw, vec):
    b, t, _ = x.shape
    tblk = min(MIX_TBLK, t)
    assert t % tblk == 0 and tblk % ROW_GROUP == 0
    const = lambda a: pl.BlockSpec(a.shape, lambda bi, ti: (0, 0), pipeline_mode=pl.Buffered(1))
    timed = lambda a: pl.BlockSpec((tblk, a.shape[1]), lambda bi, ti: (ti, 0))
    blk = pl.BlockSpec((1, tblk, D_MODEL), lambda bi, ti: (bi, ti, 0))
    return pl.pallas_call(
        functools.partial(_mix_kernel, tblk=tblk),
        name="mixers",
        grid=(b, t // tblk),
        in_specs=[blk, const(nmw), const(w), timed(cos), timed(sin), const(ccw), const(gcw), const(vec)],
        out_specs=blk,
        out_shape=jax.ShapeDtypeStruct((b, t, D_MODEL), BF16),
        scratch_shapes=[
            pltpu.VMEM((tblk, _C_END), F32),
            pltpu.VMEM((tblk + CONF_PAD, D_GROUP), F32),
            pltpu.VMEM((7, tblk + CONF_PAD, D_GROUP), F32),
            pltpu.VMEM((tblk + GDN_PAD, 3 * D_GROUP), F32),
            pltpu.VMEM((D_GROUP, D_GROUP), F32),
            pltpu.VMEM((D_GROUP, D_GROUP), F32),
            pltpu.VMEM((D_GROUP, D_GROUP), F32),
            pltpu.VMEM((D_GROUP, D_GROUP), BF16),
            pltpu.VMEM((D_GROUP, D_GROUP), F32),
            pltpu.VMEM((D_GROUP, D_GROUP), BF16),
            pltpu.VMEM((D_GROUP, D_GROUP), F32),
        ],
        compiler_params=pltpu.CompilerParams(
            dimension_semantics=("parallel", "arbitrary"), vmem_limit_bytes=VMEM_LIMIT),
    )(x, nmw, w, cos, sin, ccw, gcw, vec)


def _ffn_kernel(x_ref, m_ref, wo_ref, nfw_ref, wg_ref, wu_ref, wd_ref, nxt_ref, o_ref, *, last):
    y = x_ref[...] + _dot(m_ref[...], wo_ref[...])
    h = (y * lax.rsqrt(jnp.mean(y * y, axis=-1, keepdims=True) + 1e-6) * nfw_ref[...]).astype(BF16)
    down = None
    for f0, f1 in FF_CHUNKS:
        gate = _dot(h, wg_ref[:, f0:f1])
        up = _dot(h, wu_ref[:, f0:f1])
        part = _dot((_silu(gate) * up).astype(BF16), wd_ref[f0:f1, :])
        down = part if down is None else down + part
    y = y + down
    if last:
        y = y * lax.rsqrt(jnp.mean(y * y, axis=-1, keepdims=True) + 1e-6) * nxt_ref[...]
    o_ref[...] = y


def _out_ffn(x2d, mix, wo, nfw, wg, wu, wd, nxt_w, last):
    n = x2d.shape[0]
    tm = min(FFN_TM, n)
    row = pl.BlockSpec((tm, D_MODEL), lambda i: (i, 0))
    const = lambda shape: pl.BlockSpec(shape, lambda i: (0, 0), pipeline_mode=pl.Buffered(1))
    return pl.pallas_call(
        functools.partial(_ffn_kernel, last=last),
        name="out_ffn_last" if last else "out_ffn",
        grid=(n // tm,),
        in_specs=[row, row, const((D_MODEL, D_MODEL)), const((1, D_MODEL)), const((D_MODEL, D_FF)),
                  const((D_MODEL, D_FF)), const((D_FF, D_MODEL)), const((1, D_MODEL))],
        out_specs=row,
        out_shape=jax.ShapeDtypeStruct((n, D_MODEL), F32),
        compiler_params=pltpu.CompilerParams(
            dimension_semantics=("parallel",), vmem_limit_bytes=VMEM_LIMIT),
    )(x2d, mix, wo, nfw, wg, wu, wd, nxt_w)


def _rope_perm(w):
    d = w.shape[0]
    return w.reshape(d, N_HEADS, 2, HEAD_DIM // 2).transpose(0, 2, 1, 3).reshape(d, D_GROUP)


def _per_head_to_lanes(a):
    return jnp.repeat(a, HEAD_DIM, axis=-1)


def kernel(x, norm_mix_w, w_in, ret_norm_w, conf_conv_w, conf_conv_b, conf_ln_w, conf_ln_b, gdn_conv_w, gdn_A_log, gdn_dt_bias, gdn_norm_w, hgrn_lb_logits, hgrn_norm_w, w_out, norm_ffn_w, ffn_w_gate, ffn_w_up, ffn_w_down, final_norm_w):
    b, t, d = x.shape
    n = b * t
    g = D_GROUP
    row = lambda a: a.reshape(1, -1).astype(F32)

    pos = jnp.arange(t, dtype=F32)
    half = HEAD_DIM // 2
    inv = ROPE_BASE ** (-jnp.arange(half, dtype=F32) / half)
    ang = pos[:, None] * inv[None, :]
    cos = jnp.tile(jnp.cos(ang), (1, N_HEADS))
    sin = jnp.tile(jnp.sin(ang), (1, N_HEADS))

    lb_all = jnp.cumsum(jax.nn.softmax(hgrn_lb_logits.astype(F32), axis=0), axis=0)
    lb_all = lb_all - lb_all[0:1]

    for l in range(DEPTH):
        wl = w_in[l]
        o_gb = 10 * g
        w_all = jnp.concatenate([
            _rope_perm(wl[:, 0:g]), _rope_perm(wl[:, g:2 * g]), wl[:, 2 * g:4 * g],
            wl[:, 4 * g:6 * g],
            wl[:, 6 * g:10 * g], _per_head_to_lanes(wl[:, o_gb:o_gb + N_HEADS]),
            _per_head_to_lanes(wl[:, o_gb + N_HEADS:o_gb + 2 * N_HEADS]),
            wl[:, o_gb + 2 * N_HEADS:],
        ], axis=1).astype(BF16)
        lb = lb_all[l]
        vec = jnp.concatenate([
            row(ret_norm_w[l]), row(conf_conv_b[l]), row(conf_ln_w[l]), row(conf_ln_b[l]),
            row(_per_head_to_lanes(-jnp.exp(gdn_A_log[l].astype(F32)))),
            row(_per_head_to_lanes(gdn_dt_bias[l].astype(F32))),
            row(jnp.tile(gdn_norm_w[l], N_HEADS)),
            row(jnp.log(lb)), row(jnp.log1p(-lb)), row(1.0 - lb), row(jnp.tile(hgrn_norm_w[l], N_HEADS)),
            jnp.zeros((_VEC_ROWS - 11, g), F32)], axis=0)
        ccw = jnp.concatenate([conf_conv_w[l].astype(F32), jnp.zeros((CONF_PAD - CONF_KERNEL, g), F32)], axis=0)
        gcw = jnp.concatenate([gdn_conv_w[l].astype(F32), jnp.zeros((GDN_PAD - SHORT_CONV, 3 * g), F32)], axis=0)
        mix = _mixers(x, row(norm_mix_w[l]), w_all, cos, sin, ccw, gcw, vec)

        last = l == DEPTH - 1
        x = _out_ffn(x.reshape(n, d), mix.reshape(n, d), w_out[l].astype(BF16), row(norm_ffn_w[l]),
                     ffn_w_gate[l].astype(BF16), ffn_w_up[l].astype(BF16), ffn_w_down[l].astype(BF16),
                     row(final_norm_w), last).reshape(b, t, d)
    return x
```

```python
import functools

import numpy as np
import jax
import jax.numpy as jnp
from jax import lax
from jax.experimental import pallas as pl
from jax.experimental.pallas import tpu as pltpu

F32 = jnp.float32
BF16 = jnp.bfloat16

D_MODEL = 1024
DEPTH = 2
D_GROUP = 256
HEAD_DIM = 64
N_HEADS = 4
CHUNK = 64
SUB = 16
CONF_KERNEL = 31
CONF_PAD = 32
SHORT_CONV = 4
GDN_PAD = 8
ROPE_BASE = 10000.0
D_FF = 2816
FF_CHUNKS = ((0, 1536), (1536, D_FF))
MIX_TBLK = 512
ROW_GROUP = 256
PROJ_CHUNK = 512
FFN_TM = 1024
VMEM_LIMIT = 60 * 1024 * 1024
LOG2E = 1.4426950408889634

_LOG_GAMMA = [float(np.log1p(-(2.0 ** (-5.0 - h)))) for h in range(N_HEADS)]


def _iota(shape, dim):
    return lax.broadcasted_iota(jnp.int32, shape, dim)


def _dot(a, b):
    return jnp.dot(a, b, preferred_element_type=F32)


def _dot_nt(a, b):
    return lax.dot_general(a, b, (((1,), (1,)), ((), ())), preferred_element_type=F32)


def _dot_tn(a, b):
    return lax.dot_general(a, b, (((0,), (0,)), ((), ())), preferred_element_type=F32)


def _sigmoid(x):
    return 1.0 / (1.0 + jnp.exp2(x * -LOG2E))


def _silu(x):
    return x * _sigmoid(x)


def _log1pexp(x):
    return jnp.log(1.0 + jnp.exp2(jnp.abs(x) * -LOG2E))


def _softplus(z):
    return jnp.maximum(z, 0.0) + _log1pexp(z)


def _dot_exact_lhs(m, x):
    hi = x.astype(BF16)
    lo = (x - hi.astype(F32)).astype(BF16)
    return _dot(m, hi) + _dot(m, lo)


def _head_sum(x, ones_bd16):
    return _dot(x.astype(BF16), ones_bd16)


def _stack_heads(x16, m16):
    return jnp.concatenate([x16] * N_HEADS, axis=0) * m16


def _select_by_head(hid, vals):
    out = jnp.full(hid.shape, vals[N_HEADS - 1], F32)
    for h in range(N_HEADS - 2, -1, -1):
        out = jnp.where(hid == h, vals[h], out)
    return out


def _interleave(gens):
    done = [0] * len(gens)
    live = list(range(len(gens)))
    while live:
        i = min(live, key=lambda j: (done[j] + 0.5) / gens[j][1])
        try:
            next(gens[i][0])
            done[i] += 1
        except StopIteration:
            live.remove(i)


def _ret_stages(p_ref, c0, cos_ref, sin_ref, gnw, o_ref, oc0, s_ref, km, tblk):
    c = CHUNK
    nc = tblk // c
    half = D_GROUP // 2
    lane = _iota((c, D_GROUP), 1)
    row = _iota((c, D_GROUP), 0)
    head_qk = (lane % half) // (HEAD_DIM // 2)
    head_v = lane // HEAD_DIM
    lg_qk = _select_by_head(head_qk, _LOG_GAMMA)
    lg_v = _select_by_head(head_v, _LOG_GAMMA)
    rowf = row.astype(F32)
    diff = row - (lane % c)
    decay_mask = jnp.where(diff >= 0, jnp.exp(jnp.maximum(diff, 0).astype(F32) * lg_v), 0.0)
    q_decay = jnp.exp((rowf + 1.0) * lg_qk)
    k_decay = jnp.exp((c - 1.0 - rowf) * lg_qk) * HEAD_DIM ** -0.5
    chunk_decay = jnp.exp(float(c) * lg_v[0:1, :])

    def rope(off, r0):
        x1 = p_ref[r0:r0 + c, c0 + off:c0 + off + half]
        x2 = p_ref[r0:r0 + c, c0 + off + half:c0 + off + D_GROUP]
        cos = cos_ref[r0:r0 + c, :]
        sin = sin_ref[r0:r0 + c, :]
        return jnp.concatenate([x1 * cos - x2 * sin, x1 * sin + x2 * cos], axis=1)

    rows = [ci * c for ci in range(nc)]
    q = [rope(0, r0) for r0 in rows]
    k = [rope(D_GROUP, r0) for r0 in rows]
    v16 = [p_ref[r0:r0 + c, c0 + 2 * D_GROUP:c0 + 3 * D_GROUP].astype(BF16) for r0 in rows]
    yield
    kst = [_stack_heads((k[i] * HEAD_DIM ** -0.5).astype(BF16), km.rope_bd16[...]) for i in range(nc)]
    scores = [_dot_nt(q[i].astype(BF16), kst[i]) * decay_mask for i in range(nc)]
    yield
    intra = [_dot(scores[i].astype(BF16), _stack_heads(v16[i], km.bd16[...])) for i in range(nc)]
    kv = [_dot_tn((k[i] * k_decay).astype(BF16), v16[i]) * km.rope_state_f[...] for i in range(nc)]
    qd = [(q[i] * q_decay).astype(BF16) for i in range(nc)]
    yield
    s = s_ref[...]
    outs = []
    for i in range(nc):
        outs.append(intra[i] + _dot(qd[i], s.astype(BF16)))
        s = s * chunk_decay + kv[i]
    s_ref[...] = s
    yield
    o = jnp.concatenate(outs, axis=0)
    mu = _head_sum(o, km.bd16[...]) * (1.0 / HEAD_DIM)
    d = o - mu
    var = _head_sum(d * d, km.bd16[...]) * (1.0 / HEAD_DIM)
    y = d * lax.rsqrt(var + 1e-5) * gnw * _silu(p_ref[:, c0 + 3 * D_GROUP:c0 + 4 * D_GROUP])
    o_ref[0, :, oc0:oc0 + D_GROUP] = y.astype(o_ref.dtype)


_RET_STAGES = 5


def _conf_stages(p_ref, c0, cw_ref, cb, lnw, lnb, o_ref, oc0, buf_ref, sh_ref, tblk):
    cw = cw_ref[...]
    grp = ROW_GROUP
    first = CONF_PAD - (CONF_KERNEL - 1)
    taps_of = [[tap for tap in range(CONF_KERNEL) if (first + tap) % 8 == r] for r in range(8)]
    for r0 in range(0, tblk, grp):
        buf_ref[CONF_PAD + r0:CONF_PAD + r0 + grp, :] = (
            p_ref[r0:r0 + grp, c0:c0 + D_GROUP] * _sigmoid(p_ref[r0:r0 + grp, c0 + D_GROUP:c0 + 2 * D_GROUP]))
        yield
        for r in range(1, 8):
            n_r = first + taps_of[r][-1] - r + grp
            sh_ref[r - 1, r0:r0 + n_r, :] = buf_ref[r0 + r:r0 + r + n_r, :]
        yield
        acc = None
        for r in range(8):
            for tap in taps_of[r]:
                a0 = r0 + first + tap - r
                src = buf_ref[a0:a0 + grp, :] if r == 0 else sh_ref[r - 1, a0:a0 + grp, :]
                term = cw[tap:tap + 1, :] * src
                acc = term if acc is None else acc + term
            if r % 2 == 1:
                yield
        y = acc + cb
        mu = jnp.mean(y, axis=-1, keepdims=True)
        d = y - mu
        var = jnp.mean(d * d, axis=-1, keepdims=True)
        z = d * lax.rsqrt(var + 1e-5) * lnw + lnb
        o_ref[0, r0:r0 + grp, oc0:oc0 + D_GROUP] = _silu(z).astype(o_ref.dtype)
        yield
    buf_ref[0:CONF_PAD, :] = buf_ref[tblk:tblk + CONF_PAD, :]


_CONF_STAGES = 15


def _gdn_stages(p_ref, c0, cw_ref, nega, dtb, nw, o_ref, oc0, buf_ref, s_ref, km, tblk):
    cw = cw_ref[...]
    c = CHUNK
    nc = tblk // c
    grp = ROW_GROUP
    first = GDN_PAD - (SHORT_CONV - 1)

    lane = _iota((c, D_GROUP), 1)
    row = _iota((c, D_GROUP), 0)
    col_j = lane % c
    eye_f = jnp.where(row == col_j, 1.0, 0.0)
    incl_f = jnp.where(row >= col_j, 1.0, 0.0)
    strict_f = incl_f - eye_f
    rr = _iota((grp, grp), 0)
    cc = _iota((grp, grp), 1)
    block_tri = jnp.where((rr // c == cc // c) & (rr >= cc), 1.0, 0.0).astype(BF16)

    def bd(x):
        return _stack_heads(x.astype(BF16), km.bd16[...])

    Gc, kc, k16, vb16, kbe16, qe, lhs_aa = [], [], [], [], [], [], []
    for r0 in range(0, tblk, grp):
        buf_ref[GDN_PAD + r0:GDN_PAD + r0 + grp, :] = p_ref[r0:r0 + grp, c0:c0 + 3 * D_GROUP]
        beta = _sigmoid(p_ref[r0:r0 + grp, c0 + 4 * D_GROUP:c0 + 5 * D_GROUP])
        g = nega * _softplus(p_ref[r0:r0 + grp, c0 + 5 * D_GROUP:c0 + 6 * D_GROUP] + dtb)
        acc = None
        for tap in range(SHORT_CONV):
            term = cw[tap:tap + 1, :] * buf_ref[r0 + first + tap:r0 + first + tap + grp, :]
            acc = term if acc is None else acc + term
        qkv = _silu(acc)
        q = qkv[:, 0:D_GROUP]
        k = qkv[:, D_GROUP:2 * D_GROUP]
        v = qkv[:, 2 * D_GROUP:3 * D_GROUP]
        yield
        q = q * lax.rsqrt(_head_sum(q * q, km.bd16[...]) + 1e-6) * HEAD_DIM ** -0.5
        k = k * lax.rsqrt(_head_sum(k * k, km.bd16[...]) + 1e-6)
        G = _dot_exact_lhs(block_tri, g)
        eG = jnp.exp(G)
        kb = k * beta
        vb = (v * beta).astype(BF16)
        kbe = (kb * eG).astype(BF16)
        qeg = q * eG
        kg16 = k.astype(BF16)
        for j in range(0, grp, c):
            Gc.append(G[j:j + c])
            kc.append(k[j:j + c])
            k16.append(kg16[j:j + c])
            vb16.append(vb[j:j + c])
            kbe16.append(kbe[j:j + c])
            qe.append(qeg[j:j + c])
            lhs_aa.append(jnp.concatenate([kb[j:j + c], q[j:j + c]], axis=0).astype(BF16))
        yield
    buf_ref[0:GDN_PAD, :] = buf_ref[tblk:tblk + GDN_PAD, :]

    g_last = [Gc[i][c - 1:c, :] for i in range(nc)]
    L = []
    for i in range(nc):
        g_row = jnp.sum(eye_f * Gc[i], axis=0, keepdims=True)
        L.append(jnp.exp(jnp.minimum(Gc[i] - g_row, 0.0)))
    yield
    aa = [_dot_nt(lhs_aa[i], bd(k16[i])) for i in range(nc)]
    A = [aa[i][0:c] * (L[i] * strict_f) for i in range(nc)]
    attn = [(aa[i][c:2 * c] * (L[i] * incl_f)).astype(BF16) for i in range(nc)]
    yield
    T = [eye_f - A[i] for i in range(nc)]
    P = [_dot(A[i].astype(BF16), bd(A[i])) for i in range(nc)]
    yield
    for it in range(5):
        if it < 4:
            tp = [_dot(jnp.concatenate([T[i], P[i]], axis=0).astype(BF16), bd(P[i])) for i in range(nc)]
            T = [T[i] + tp[i][0:c] for i in range(nc)]
            P = [tp[i][c:2 * c] for i in range(nc)]
        else:
            T = [T[i] + _dot(T[i].astype(BF16), bd(P[i])) for i in range(nc)]
        yield
    T16 = [T[i].astype(BF16) for i in range(nc)]
    u = [_dot(T16[i], bd(vb16[i])) for i in range(nc)]
    w = [_dot(T16[i], bd(kbe16[i])) for i in range(nc)]
    lhs_ws = [jnp.concatenate([w[i], qe[i]], axis=0).astype(BF16) for i in range(nc)]
    k_dec = [(kc[i] * jnp.exp(g_last[i] - Gc[i])).astype(BF16) for i in range(nc)]
    e_last = [jnp.exp(g_last[i]) for i in range(nc)]
    yield
    s = s_ref[...]
    outs = []
    for i in range(nc):
        ws = _dot(lhs_ws[i], s.astype(BF16))
        v_new = u[i] - ws[0:c]
        outs.append(ws[c:2 * c] + _dot(attn[i], bd(v_new)))
        s = s * e_last[i] + _dot_tn(k_dec[i], v_new.astype(BF16)) * km.bd_f[...]
        yield
    s_ref[...] = s
    o = jnp.concatenate(outs, axis=0)
    ms = _head_sum(o * o, km.bd16[...]) * (1.0 / HEAD_DIM)
    y = o * lax.rsqrt(ms + 1e-6) * nw * _silu(p_ref[:, c0 + 3 * D_GROUP:c0 + 4 * D_GROUP])
    o_ref[0, :, oc0:oc0 + D_GROUP] = y.astype(o_ref.dtype)


_GDN_STAGES = 22


def _hgrn_stages(p_ref, c0, log_lb, l1m, oml, nw, o_ref, oc0, st_ref, km, tblk):
    grp = ROW_GROUP
    nsub = grp // SUB
    hsub = SUB // 2
    rr = _iota((grp, grp), 0)
    cc = _iota((grp, grp), 1)
    block_tri = jnp.where((rr // SUB == cc // SUB) & (rr >= cc), 1.0, 0.0).astype(BF16)
    row_in_sub = _iota((nsub, SUB, D_GROUP), 1)
    row_in_half = _iota((nsub, hsub, D_GROUP), 1) + hsub

    groups, qg_all, kv_all, e_end_all = [], [], [], []
    for r0 in range(0, tblk, grp):
        q = p_ref[r0:r0 + grp, c0:c0 + D_GROUP]
        x = p_ref[r0:r0 + grp, c0 + D_GROUP:c0 + 2 * D_GROUP]
        v = p_ref[r0:r0 + grp, c0 + 2 * D_GROUP:c0 + 3 * D_GROUP]
        b_ = l1m + (jnp.minimum(x, 0.0) - _log1pexp(x))
        log_f = jnp.maximum(log_lb, b_) + _log1pexp(log_lb - b_)
        k = oml * _sigmoid(-x)
        G = _dot_exact_lhs(block_tri, log_f)
        G3 = G.reshape(nsub, SUB, D_GROUP)
        k3 = k.reshape(nsub, SUB, D_GROUP)
        groups.append((G3 * LOG2E, q.reshape(nsub, SUB, D_GROUP), k3, v.reshape(nsub, SUB, D_GROUP)))
        eG = jnp.exp(G)
        qg = (q * eG).astype(BF16)
        k_end = (k3 * jnp.exp(G3[:, SUB - 1:SUB, :] - G3)).reshape(grp, D_GROUP).astype(BF16)
        v16 = v.astype(BF16)
        for bi in range(nsub):
            lo = bi * SUB
            qg_all.append(qg[lo:lo + SUB])
            e_end_all.append(eG[lo + SUB - 1:lo + SUB, :])
            kv_all.append(_dot_tn(v16[lo:lo + SUB], k_end[lo:lo + SUB]) * km.bd_f[...])
        yield

    st = st_ref[...]
    inter = []
    for bi in range(len(kv_all)):
        inter.append(_dot_nt(qg_all[bi], st.astype(BF16)))
        st = st * e_end_all[bi] + kv_all[bi]
        if bi % 4 == 3:
            yield
    st_ref[...] = st

    intra = []
    for G2, q3, k3, v3 in groups:
        full = None
        for j in range(hsub):
            e = jnp.exp2(jnp.where(row_in_sub >= j, G2 - G2[:, j:j + 1, :], -1e30))
            z = q3 * e * k3[:, j:j + 1, :]
            sc = _head_sum(z.reshape(grp, D_GROUP), km.bd16[...])
            term = sc.reshape(nsub, SUB, D_GROUP) * v3[:, j:j + 1, :]
            full = term if full is None else full + term
            yield
        Gh = G2[:, hsub:SUB, :]
        qh = q3[:, hsub:SUB, :]
        upper = None
        for j in range(hsub, SUB):
            e = jnp.exp2(jnp.where(row_in_half >= j, Gh - G2[:, j:j + 1, :], -1e30))
            z = qh * e * k3[:, j:j + 1, :]
            sc = _head_sum(z.reshape(grp // 2, D_GROUP), km.bd16[...])
            term = sc.reshape(nsub, hsub, D_GROUP) * v3[:, j:j + 1, :]
            upper = term if upper is None else upper + term
            yield
        intra.append(jnp.concatenate([full[:, 0:hsub, :], full[:, hsub:SUB, :] + upper],
                                     axis=1).reshape(grp, D_GROUP))
    o = jnp.concatenate(intra, axis=0) + jnp.concatenate(inter, axis=0)
    ms = _head_sum(o * o, km.bd16[...]) * (1.0 / HEAD_DIM)
    y = o * lax.rsqrt(ms + 1e-6) * nw * _silu(p_ref[:, c0 + 3 * D_GROUP:c0 + 4 * D_GROUP])
    o_ref[0, :, oc0:oc0 + D_GROUP] = y.astype(o_ref.dtype)


_HGRN_STAGES = 43


_C_RET, _C_CONF, _C_GDN, _C_HGRN, _C_END = 0, 4 * D_GROUP, 6 * D_GROUP, 12 * D_GROUP, 16 * D_GROUP
_VEC_ROWS = 16


class _HeadMasks:
    def __init__(self, bd16, bd_f, rope_bd16, rope_state_f):
        self.bd16 = bd16
        self.bd_f = bd_f
        self.rope_bd16 = rope_bd16
        self.rope_state_f = rope_state_f


def _project(x_ref, nmw_ref, w_ref, p_ref, tblk):
    for r0 in range(0, tblk, ROW_GROUP):
        x = x_ref[0, r0:r0 + ROW_GROUP, :]
        h = (x * lax.rsqrt(jnp.mean(x * x, axis=-1, keepdims=True) + 1e-6) * nmw_ref[...]).astype(BF16)
        for c0 in range(0, _C_END, PROJ_CHUNK):
            p_ref[r0:r0 + ROW_GROUP, c0:c0 + PROJ_CHUNK] = _dot(h, w_ref[:, c0:c0 + PROJ_CHUNK])


def _mix_kernel(x_ref, nmw_ref, w_ref, cos_ref, sin_ref, ccw_ref, gcw_ref, vec_ref, o_ref,
                p_ref, cbuf_ref, csh_ref, gbuf_ref, sret_ref, sgdn_ref, shg_ref,
                bd16_ref, bdf_ref, rbd16_ref, rsf_ref, *, tblk):
    @pl.when(pl.program_id(1) == 0)
    def _():
        r = _iota((D_GROUP, D_GROUP), 0)
        l = _iota((D_GROUP, D_GROUP), 1)
        rope_head = lambda i: (i % (D_GROUP // 2)) // (HEAD_DIM // 2)
        bd = jnp.where(r // HEAD_DIM == l // HEAD_DIM, 1.0, 0.0)
        bdf_ref[...] = bd
        bd16_ref[...] = bd.astype(BF16)
        rbd16_ref[...] = jnp.where(r // HEAD_DIM == rope_head(l), 1.0, 0.0).astype(BF16)
        rsf_ref[...] = jnp.where(rope_head(r) == l // HEAD_DIM, 1.0, 0.0)
        sret_ref[...] = jnp.zeros_like(sret_ref)
        sgdn_ref[...] = jnp.zeros_like(sgdn_ref)
        shg_ref[...] = jnp.zeros_like(shg_ref)
        cbuf_ref[0:CONF_PAD, :] = jnp.zeros((CONF_PAD, D_GROUP), F32)
        gbuf_ref[0:GDN_PAD, :] = jnp.zeros((GDN_PAD, 3 * D_GROUP), F32)

    km = _HeadMasks(bd16_ref, bdf_ref, rbd16_ref, rsf_ref)
    _project(x_ref, nmw_ref, w_ref, p_ref, tblk)
    vec = lambda i: vec_ref[i:i + 1, :]
    g = D_GROUP
    _interleave([
        (_ret_stages(p_ref, _C_RET, cos_ref, sin_ref, vec(0), o_ref, 0, sret_ref, km, tblk), _RET_STAGES),
        (_conf_stages(p_ref, _C_CONF, ccw_ref, vec(1), vec(2), vec(3), o_ref, g, cbuf_ref, csh_ref, tblk),
         _CONF_STAGES),
        (_gdn_stages(p_ref, _C_GDN, gcw_ref, vec(4), vec(5), vec(6), o_ref, 2 * g, gbuf_ref, sgdn_ref, km, tblk),
         _GDN_STAGES),
        (_hgrn_stages(p_ref, _C_HGRN, vec(7), vec(8), vec(9), vec(10), o_ref, 3 * g, shg_ref, km, tblk),
         _HGRN_STAGES),
    ])


def _mixers(x, nmw, w, cos, sin, ccw, gcw, vec):
    b, t, _ = x.shape
    tblk = min(MIX_TBLK, t)
    assert t % tblk == 0 and tblk % ROW_GROUP == 0
    const = lambda a: pl.BlockSpec(a.shape, lambda bi, ti: (0, 0), pipeline_mode=pl.Buffered(1))
    timed = lambda a: pl.BlockSpec((tblk, a.shape[1]), lambda bi, ti: (ti, 0))
    blk = pl.BlockSpec((1, tblk, D_MODEL), lambda bi, ti: (bi, ti, 0))
    return pl.pallas_call(
        functools.partial(_mix_kernel, tblk=tblk),
        name="mixers",
        grid=(b, t // tblk),
        in_specs=[blk, const(nmw), const(w), timed(cos), timed(sin), const(ccw), const(gcw), const(vec)],
        out_specs=blk,
        out_shape=jax.ShapeDtypeStruct((b, t, D_MODEL), BF16),
        scratch_shapes=[
            pltpu.VMEM((tblk, _C_END), F32),
            pltpu.VMEM((tblk + CONF_PAD, D_GROUP), F32),
            pltpu.VMEM((7, tblk + CONF_PAD, D_GROUP), F32),
            pltpu.VMEM((tblk + GDN_PAD, 3 * D_GROUP), F32),
            pltpu.VMEM((D_GROUP, D_GROUP), F32),
            pltpu.VMEM((D_GROUP, D_GROUP), F32),
            pltpu.VMEM((D_GROUP, D_GROUP), F32),
            pltpu.VMEM((D_GROUP, D_GROUP), BF16),
            pltpu.VMEM((D_GROUP, D_GROUP), F32),
            pltpu.VMEM((D_GROUP, D_GROUP), BF16),
            pltpu.VMEM((D_GROUP, D_GROUP), F32),
        ],
        compiler_params=pltpu.CompilerParams(
            dimension_semantics=("parallel", "arbitrary"), vmem_limit_bytes=VMEM_LIMIT),
    )(x, nmw, w, cos, sin, ccw, gcw, vec)


def _ffn_kernel(x_ref, m_ref, wo_ref, nfw_ref, wg_ref, wu_ref, wd_ref, nxt_ref, o_ref, *, last):
    y = x_ref[...] + _dot(m_ref[...], wo_ref[...])
    h = (y * lax.rsqrt(jnp.mean(y * y, axis=-1, keepdims=True) + 1e-6) * nfw_ref[...]).astype(BF16)
    down = None
    for f0, f1 in FF_CHUNKS:
        gate = _dot(h, wg_ref[:, f0:f1])
        up = _dot(h, wu_ref[:, f0:f1])
        part = _dot((_silu(gate) * up).astype(BF16), wd_ref[f0:f1, :])
        down = part if down is None else down + part
    y = y + down
    if last:
        y = y * lax.rsqrt(jnp.mean(y * y, axis=-1, keepdims=True) + 1e-6) * nxt_ref[...]
    o_ref[...] = y


def _out_ffn(x2d, mix, wo, nfw, wg, wu, wd, nxt_w, last):
    n = x2d.shape[0]
    tm = min(FFN_TM, n)
    row = pl.BlockSpec((tm, D_MODEL), lambda i: (i, 0))
    const = lambda shape: pl.BlockSpec(shape, lambda i: (0, 0), pipeline_mode=pl.Buffered(1))
    return pl.pallas_call(
        functools.partial(_ffn_kernel, last=last),
        name="out_ffn_last" if last else "out_ffn",
        grid=(n // tm,),
        in_specs=[row, row, const((D_MODEL, D_MODEL)), const((1, D_MODEL)), const((D_MODEL, D_FF)),
                  const((D_MODEL, D_FF)), const((D_FF, D_MODEL)), const((1, D_MODEL))],
        out_specs=row,
        out_shape=jax.ShapeDtypeStruct((n, D_MODEL), F32),
        compiler_params=pltpu.CompilerParams(
            dimension_semantics=("parallel",), vmem_limit_bytes=VMEM_LIMIT),
    )(x2d, mix, wo, nfw, wg, wu, wd, nxt_w)


def _rope_perm(w):
    lead = w.shape[:-1]
    return w.reshape(*lead, N_HEADS, 2, HEAD_DIM // 2).swapaxes(-3, -2).reshape(*lead, D_GROUP)


def _per_head_to_lanes(a):
    return jnp.repeat(a, HEAD_DIM, axis=-1)


def kernel(x, norm_mix_w, w_in, ret_norm_w, conf_conv_w, conf_conv_b, conf_ln_w, conf_ln_b, gdn_conv_w, gdn_A_log, gdn_dt_bias, gdn_norm_w, hgrn_lb_logits, hgrn_norm_w, w_out, norm_ffn_w, ffn_w_gate, ffn_w_up, ffn_w_down, final_norm_w):
    b, t, d = x.shape
    n = b * t
    g = D_GROUP
    f32 = lambda a: a.astype(F32)
    rows = lambda a: f32(a)[:, None, :]

    pos = jnp.arange(t, dtype=F32)
    half = HEAD_DIM // 2
    inv = ROPE_BASE ** (-jnp.arange(half, dtype=F32) / half)
    ang = pos[:, None] * inv[None, :]
    cos = jnp.tile(jnp.cos(ang), (1, N_HEADS))
    sin = jnp.tile(jnp.sin(ang), (1, N_HEADS))

    lb = jnp.cumsum(jax.nn.softmax(f32(hgrn_lb_logits), axis=0), axis=0)
    lb = lb - lb[0:1]

    o_gb = 10 * g
    w_all = jnp.concatenate([
        _rope_perm(w_in[..., 0:g]), _rope_perm(w_in[..., g:2 * g]), w_in[..., 2 * g:4 * g],
        w_in[..., 4 * g:6 * g],
        w_in[..., 6 * g:10 * g], _per_head_to_lanes(w_in[..., o_gb:o_gb + N_HEADS]),
        _per_head_to_lanes(w_in[..., o_gb + N_HEADS:o_gb + 2 * N_HEADS]),
        w_in[..., o_gb + 2 * N_HEADS:],
    ], axis=-1).astype(BF16)
    vec = jnp.concatenate([
        rows(ret_norm_w), rows(conf_conv_b), rows(conf_ln_w), rows(conf_ln_b),
        rows(_per_head_to_lanes(-jnp.exp(f32(gdn_A_log)))), rows(_per_head_to_lanes(gdn_dt_bias)),
        rows(jnp.tile(gdn_norm_w, (1, N_HEADS))),
        rows(jnp.log(lb)), rows(jnp.log1p(-lb)), rows(1.0 - lb), rows(jnp.tile(hgrn_norm_w, (1, N_HEADS))),
        jnp.zeros((DEPTH, _VEC_ROWS - 11, g), F32)], axis=1)
    ccw = jnp.concatenate([f32(conf_conv_w), jnp.zeros((DEPTH, CONF_PAD - CONF_KERNEL, g), F32)], axis=1)
    gcw = jnp.concatenate([f32(gdn_conv_w), jnp.zeros((DEPTH, GDN_PAD - SHORT_CONV, 3 * g), F32)], axis=1)
    wo, wg, wu, wd = (a.astype(BF16) for a in (w_out, ffn_w_gate, ffn_w_up, ffn_w_down))
    nmw, nfw = rows(norm_mix_w), rows(norm_ffn_w)

    for l in range(DEPTH):
        mix = _mixers(x, nmw[l], w_all[l], cos, sin, ccw[l], gcw[l], vec[l])
        x = _out_ffn(x.reshape(n, d), mix.reshape(n, d), wo[l], nfw[l], wg[l], wu[l], wd[l],
                     f32(final_norm_w).reshape(1, d), l == DEPTH - 1).reshape(b, t, d)
    return x
```

```python
import functools

import numpy as np
import jax
import jax.numpy as jnp
from jax import lax
from jax.experimental import pallas as pl
from jax.experimental.pallas import tpu as pltpu

F32 = jnp.float32
BF16 = jnp.bfloat16

D_MODEL = 1024
DEPTH = 2
D_GROUP = 256
HEAD_DIM = 64
N_HEADS = 4
CHUNK = 64
SUB = 16
HGRN_STATE_CHUNKS = 2
CONF_KERNEL = 31
CONF_PAD = 32
SHORT_CONV = 4
GDN_PAD = 8
ROPE_BASE = 10000.0
D_FF = 2816
FF_CHUNKS = ((0, 1536), (1536, D_FF))
MIX_TBLK = 512
ROW_GROUP = 256
PROJ_CHUNK = 512
FFN_TM = 1024
VMEM_LIMIT = 60 * 1024 * 1024
LOG2E = 1.4426950408889634

_LOG_GAMMA = [float(np.log1p(-(2.0 ** (-5.0 - h)))) for h in range(N_HEADS)]


def _iota(shape, dim):
    return lax.broadcasted_iota(jnp.int32, shape, dim)


def _dot(a, b):
    return jnp.dot(a, b, preferred_element_type=F32)


def _dot_nt(a, b):
    return lax.dot_general(a, b, (((1,), (1,)), ((), ())), preferred_element_type=F32)


def _dot_tn(a, b):
    return lax.dot_general(a, b, (((0,), (0,)), ((), ())), preferred_element_type=F32)


def _sigmoid(x):
    return 1.0 / (1.0 + jnp.exp2(x * -LOG2E))


def _silu(x):
    return x * _sigmoid(x)


def _log1pexp(x):
    return jnp.log(1.0 + jnp.exp2(jnp.abs(x) * -LOG2E))


def _softplus(z):
    return jnp.maximum(z, 0.0) + _log1pexp(z)


def _dot_exact_lhs(m, x):
    hi = x.astype(BF16)
    lo = (x - hi.astype(F32)).astype(BF16)
    return _dot(m, hi) + _dot(m, lo)


def _head_sum(x, ones_bd16):
    return _dot(x.astype(BF16), ones_bd16)


def _stack_heads(x16, m16):
    return jnp.concatenate([x16] * N_HEADS, axis=0) * m16


def _select_by_head(hid, vals):
    out = jnp.full(hid.shape, vals[N_HEADS - 1], F32)
    for h in range(N_HEADS - 2, -1, -1):
        out = jnp.where(hid == h, vals[h], out)
    return out


def _interleave(gens):
    done = [0] * len(gens)
    live = list(range(len(gens)))
    while live:
        i = min(live, key=lambda j: (done[j] + 0.5) / gens[j][1])
        try:
            next(gens[i][0])
            done[i] += 1
        except StopIteration:
            live.remove(i)


def _ret_stages(p_ref, c0, cos_ref, sin_ref, gnw, o_ref, oc0, s_ref, km, tblk):
    c = CHUNK
    nc = tblk // c
    half = D_GROUP // 2
    lane = _iota((c, D_GROUP), 1)
    row = _iota((c, D_GROUP), 0)
    head_qk = (lane % half) // (HEAD_DIM // 2)
    head_v = lane // HEAD_DIM
    lg_qk = _select_by_head(head_qk, _LOG_GAMMA)
    lg_v = _select_by_head(head_v, _LOG_GAMMA)
    rowf = row.astype(F32)
    diff = row - (lane % c)
    decay_mask = jnp.where(diff >= 0, jnp.exp(jnp.maximum(diff, 0).astype(F32) * lg_v), 0.0)
    q_decay = jnp.exp((rowf + 1.0) * lg_qk)
    k_decay = jnp.exp((c - 1.0 - rowf) * lg_qk) * HEAD_DIM ** -0.5
    chunk_decay = jnp.exp(float(c) * lg_v[0:1, :])

    def rope(off, r0):
        x1 = p_ref[r0:r0 + c, c0 + off:c0 + off + half]
        x2 = p_ref[r0:r0 + c, c0 + off + half:c0 + off + D_GROUP]
        cos = cos_ref[r0:r0 + c, :]
        sin = sin_ref[r0:r0 + c, :]
        return jnp.concatenate([x1 * cos - x2 * sin, x1 * sin + x2 * cos], axis=1)

    rows = [ci * c for ci in range(nc)]
    q = [rope(0, r0) for r0 in rows]
    k = [rope(D_GROUP, r0) for r0 in rows]
    v16 = [p_ref[r0:r0 + c, c0 + 2 * D_GROUP:c0 + 3 * D_GROUP].astype(BF16) for r0 in rows]
    yield
    kst = [_stack_heads((k[i] * HEAD_DIM ** -0.5).astype(BF16), km.rope_bd16[...]) for i in range(nc)]
    scores = [_dot_nt(q[i].astype(BF16), kst[i]) * decay_mask for i in range(nc)]
    yield
    intra = [_dot(scores[i].astype(BF16), _stack_heads(v16[i], km.bd16[...])) for i in range(nc)]
    kv = [_dot_tn((k[i] * k_decay).astype(BF16), v16[i]) * km.rope_state_f[...] for i in range(nc)]
    qd = [(q[i] * q_decay).astype(BF16) for i in range(nc)]
    yield
    s = s_ref[...]
    outs = []
    for i in range(nc):
        outs.append(intra[i] + _dot(qd[i], s.astype(BF16)))
        s = s * chunk_decay + kv[i]
    s_ref[...] = s
    yield
    o = jnp.concatenate(outs, axis=0)
    mu = _head_sum(o, km.bd16[...]) * (1.0 / HEAD_DIM)
    d = o - mu
    var = _head_sum(d * d, km.bd16[...]) * (1.0 / HEAD_DIM)
    y = d * lax.rsqrt(var + 1e-5) * gnw * _silu(p_ref[:, c0 + 3 * D_GROUP:c0 + 4 * D_GROUP])
    o_ref[0, :, oc0:oc0 + D_GROUP] = y.astype(o_ref.dtype)


_RET_STAGES = 5


def _conf_stages(p_ref, c0, cw_ref, cb, lnw, lnb, o_ref, oc0, buf_ref, sh_ref, tblk):
    cw = cw_ref[...]
    grp = ROW_GROUP
    first = CONF_PAD - (CONF_KERNEL - 1)
    taps_of = [[tap for tap in range(CONF_KERNEL) if (first + tap) % 8 == r] for r in range(8)]
    for r0 in range(0, tblk, grp):
        buf_ref[CONF_PAD + r0:CONF_PAD + r0 + grp, :] = (
            p_ref[r0:r0 + grp, c0:c0 + D_GROUP] * _sigmoid(p_ref[r0:r0 + grp, c0 + D_GROUP:c0 + 2 * D_GROUP]))
        yield
        for r in range(1, 8):
            n_r = first + taps_of[r][-1] - r + grp
            sh_ref[r - 1, r0:r0 + n_r, :] = buf_ref[r0 + r:r0 + r + n_r, :]
        yield
        acc = None
        for r in range(8):
            for tap in taps_of[r]:
                a0 = r0 + first + tap - r
                src = buf_ref[a0:a0 + grp, :] if r == 0 else sh_ref[r - 1, a0:a0 + grp, :]
                term = cw[tap:tap + 1, :] * src
                acc = term if acc is None else acc + term
            if r % 2 == 1:
                yield
        y = acc + cb
        mu = jnp.mean(y, axis=-1, keepdims=True)
        d = y - mu
        var = jnp.mean(d * d, axis=-1, keepdims=True)
        z = d * lax.rsqrt(var + 1e-5) * lnw + lnb
        o_ref[0, r0:r0 + grp, oc0:oc0 + D_GROUP] = _silu(z).astype(o_ref.dtype)
        yield
    buf_ref[0:CONF_PAD, :] = buf_ref[tblk:tblk + CONF_PAD, :]


_CONF_STAGES = 15


def _gdn_stages(p_ref, c0, cw_ref, nega, dtb, nw, o_ref, oc0, buf_ref, s_ref, km, tblk):
    cw = cw_ref[...]
    c = CHUNK
    nc = tblk // c
    grp = ROW_GROUP
    first = GDN_PAD - (SHORT_CONV - 1)

    lane = _iota((c, D_GROUP), 1)
    row = _iota((c, D_GROUP), 0)
    col_j = lane % c
    eye_f = jnp.where(row == col_j, 1.0, 0.0)
    incl_f = jnp.where(row >= col_j, 1.0, 0.0)
    strict_f = incl_f - eye_f
    rr = _iota((grp, grp), 0)
    cc = _iota((grp, grp), 1)
    block_tri = jnp.where((rr // c == cc // c) & (rr >= cc), 1.0, 0.0).astype(BF16)

    def bd(x):
        return _stack_heads(x.astype(BF16), km.bd16[...])

    Gc, kc, k16, vb16, kbe16, qe, lhs_aa = [], [], [], [], [], [], []
    for r0 in range(0, tblk, grp):
        buf_ref[GDN_PAD + r0:GDN_PAD + r0 + grp, :] = p_ref[r0:r0 + grp, c0:c0 + 3 * D_GROUP]
        beta = _sigmoid(p_ref[r0:r0 + grp, c0 + 4 * D_GROUP:c0 + 5 * D_GROUP])
        g = nega * _softplus(p_ref[r0:r0 + grp, c0 + 5 * D_GROUP:c0 + 6 * D_GROUP] + dtb)
        acc = None
        for tap in range(SHORT_CONV):
            term = cw[tap:tap + 1, :] * buf_ref[r0 + first + tap:r0 + first + tap + grp, :]
            acc = term if acc is None else acc + term
        qkv = _silu(acc)
        q = qkv[:, 0:D_GROUP]
        k = qkv[:, D_GROUP:2 * D_GROUP]
        v = qkv[:, 2 * D_GROUP:3 * D_GROUP]
        yield
        q = q * lax.rsqrt(_head_sum(q * q, km.bd16[...]) + 1e-6) * HEAD_DIM ** -0.5
        k = k * lax.rsqrt(_head_sum(k * k, km.bd16[...]) + 1e-6)
        G = _dot_exact_lhs(block_tri, g)
        eG = jnp.exp(G)
        kb = k * beta
        vb = (v * beta).astype(BF16)
        kbe = (kb * eG).astype(BF16)
        qeg = q * eG
        kg16 = k.astype(BF16)
        for j in range(0, grp, c):
            Gc.append(G[j:j + c])
            kc.append(k[j:j + c])
            k16.append(kg16[j:j + c])
            vb16.append(vb[j:j + c])
            kbe16.append(kbe[j:j + c])
            qe.append(qeg[j:j + c])
            lhs_aa.append(jnp.concatenate([kb[j:j + c], q[j:j + c]], axis=0).astype(BF16))
        yield
    buf_ref[0:GDN_PAD, :] = buf_ref[tblk:tblk + GDN_PAD, :]

    g_last = [Gc[i][c - 1:c, :] for i in range(nc)]
    L = []
    for i in range(nc):
        g_row = jnp.sum(eye_f * Gc[i], axis=0, keepdims=True)
        L.append(jnp.exp(jnp.minimum(Gc[i] - g_row, 0.0)))
    yield
    aa = [_dot_nt(lhs_aa[i], bd(k16[i])) for i in range(nc)]
    A = [aa[i][0:c] * (L[i] * strict_f) for i in range(nc)]
    attn = [(aa[i][c:2 * c] * (L[i] * incl_f)).astype(BF16) for i in range(nc)]
    yield
    T = [eye_f - A[i] for i in range(nc)]
    P = [_dot(A[i].astype(BF16), bd(A[i])) for i in range(nc)]
    yield
    for it in range(5):
        if it < 4:
            tp = [_dot(jnp.concatenate([T[i], P[i]], axis=0).astype(BF16), bd(P[i])) for i in range(nc)]
            T = [T[i] + tp[i][0:c] for i in range(nc)]
            P = [tp[i][c:2 * c] for i in range(nc)]
        else:
            T = [T[i] + _dot(T[i].astype(BF16), bd(P[i])) for i in range(nc)]
        yield
    T16 = [T[i].astype(BF16) for i in range(nc)]
    u = [_dot(T16[i], bd(vb16[i])) for i in range(nc)]
    w = [_dot(T16[i], bd(kbe16[i])) for i in range(nc)]
    lhs_ws = [jnp.concatenate([w[i], qe[i]], axis=0).astype(BF16) for i in range(nc)]
    k_dec = [(kc[i] * jnp.exp(g_last[i] - Gc[i])).astype(BF16) for i in range(nc)]
    e_last = [jnp.exp(g_last[i]) for i in range(nc)]
    yield
    s = s_ref[...]
    outs = []
    for i in range(nc):
        ws = _dot(lhs_ws[i], s.astype(BF16))
        v_new = u[i] - ws[0:c]
        outs.append(ws[c:2 * c] + _dot(attn[i], bd(v_new)))
        s = s * e_last[i] + _dot_tn(k_dec[i], v_new.astype(BF16)) * km.bd_f[...]
        yield
    s_ref[...] = s
    o = jnp.concatenate(outs, axis=0)
    ms = _head_sum(o * o, km.bd16[...]) * (1.0 / HEAD_DIM)
    y = o * lax.rsqrt(ms + 1e-6) * nw * _silu(p_ref[:, c0 + 3 * D_GROUP:c0 + 4 * D_GROUP])
    o_ref[0, :, oc0:oc0 + D_GROUP] = y.astype(o_ref.dtype)


_GDN_STAGES = 22


def _hgrn_stages(p_ref, c0, log_lb, l1m, oml, nw, o_ref, oc0, st_ref, km, tblk):
    grp = ROW_GROUP
    nsub = grp // SUB
    hsub = SUB // 2
    rr = _iota((grp, grp), 0)
    cc = _iota((grp, grp), 1)
    block_tri = jnp.where((rr // SUB == cc // SUB) & (rr >= cc), 1.0, 0.0).astype(BF16)
    row_in_sub = _iota((nsub, SUB, D_GROUP), 1)
    row_in_half = _iota((nsub, hsub, D_GROUP), 1) + hsub

    bd16_sub = jnp.concatenate([km.bd16[h * HEAD_DIM:h * HEAD_DIM + SUB, :] for h in range(N_HEADS)], axis=0)

    def stack_sub(x16):
        return jnp.concatenate([x16] * N_HEADS, axis=0) * bd16_sub

    groups, lhs_all, kv_all, e_all, cross_all = [], [], [], [], []
    for r0 in range(0, tblk, grp):
        q = p_ref[r0:r0 + grp, c0:c0 + D_GROUP]
        x = p_ref[r0:r0 + grp, c0 + D_GROUP:c0 + 2 * D_GROUP]
        v = p_ref[r0:r0 + grp, c0 + 2 * D_GROUP:c0 + 3 * D_GROUP]
        b_ = l1m + (jnp.minimum(x, 0.0) - _log1pexp(x))
        log_f = jnp.maximum(log_lb, b_) + _log1pexp(log_lb - b_)
        k = oml * _sigmoid(-x)
        G = _dot_exact_lhs(block_tri, log_f)
        G3 = G.reshape(nsub, SUB, D_GROUP)
        k3 = k.reshape(nsub, SUB, D_GROUP)
        groups.append((G3 * LOG2E, q.reshape(nsub, SUB, D_GROUP), k3, v.reshape(nsub, SUB, D_GROUP)))
        eG = jnp.exp(G)
        qg = q * eG
        k_end = (k3 * jnp.exp(G3[:, SUB - 1:SUB, :] - G3)).reshape(grp, D_GROUP)
        v16 = v.astype(BF16)
        ns = HGRN_STATE_CHUNKS
        for lo in range(0, grp, ns * SUB):
            rows = [slice(lo + i * SUB, lo + (i + 1) * SUB) for i in range(ns)]
            e_c = [eG[r.stop - 1:r.stop, :] for r in rows]
            v_st = [stack_sub(v16[r]) for r in rows[:-1]]
            lhs, cross = [qg[rows[0]]], [None]
            before = None
            for i in range(1, ns):
                before = e_c[i - 1] if before is None else before * e_c[i - 1]
                lhs.append(qg[rows[i]] * before)
                k_st, between = [], None
                for j in range(i - 1, -1, -1):
                    kj = k_end[rows[j]] if between is None else k_end[rows[j]] * between
                    k_st.insert(0, stack_sub(kj.astype(BF16)))
                    between = e_c[j] if between is None else between * e_c[j]
                sc = _dot_nt(qg[rows[i]].astype(BF16), jnp.concatenate(k_st, axis=0))
                cross.append(_dot(sc.astype(BF16), jnp.concatenate(v_st[:i], axis=0)))
            k_all, after = [k_end[rows[ns - 1]]], None
            for i in range(ns - 2, -1, -1):
                after = e_c[i + 1] if after is None else after * e_c[i + 1]
                k_all.insert(0, k_end[rows[i]] * after)
            lhs_all.append(jnp.concatenate(lhs, axis=0).astype(BF16))
            cross_all.append(cross)
            kv_all.append(_dot_tn(v16[lo:lo + ns * SUB], jnp.concatenate(k_all, axis=0).astype(BF16))
                          * km.bd_f[...])
            e_all.append(before * e_c[ns - 1])
        yield

    st = st_ref[...]
    inter = []
    for bi in range(len(kv_all)):
        from_state = _dot_nt(lhs_all[bi], st.astype(BF16))
        for i, cross in enumerate(cross_all[bi]):
            part = from_state[i * SUB:(i + 1) * SUB]
            inter.append(part if cross is None else part + cross)
        st = st * e_all[bi] + kv_all[bi]
        if (bi + 1) % max(1, len(kv_all) // 8) == 0:
            yield
    st_ref[...] = st

    intra = []
    for G2, q3, k3, v3 in groups:
        full = None
        for j in range(hsub):
            e = jnp.exp2(jnp.where(row_in_sub >= j, G2 - G2[:, j:j + 1, :], -1e30))
            z = q3 * e * k3[:, j:j + 1, :]
            sc = _head_sum(z.reshape(grp, D_GROUP), km.bd16[...])
            term = sc.reshape(nsub, SUB, D_GROUP) * v3[:, j:j + 1, :]
            full = term if full is None else full + term
            yield
        Gh = G2[:, hsub:SUB, :]
        qh = q3[:, hsub:SUB, :]
        upper = None
        for j in range(hsub, SUB):
            e = jnp.exp2(jnp.where(row_in_half >= j, Gh - G2[:, j:j + 1, :], -1e30))
            z = qh * e * k3[:, j:j + 1, :]
            sc = _head_sum(z.reshape(grp // 2, D_GROUP), km.bd16[...])
            term = sc.reshape(nsub, hsub, D_GROUP) * v3[:, j:j + 1, :]
            upper = term if upper is None else upper + term
            yield
        intra.append(jnp.concatenate([full[:, 0:hsub, :], full[:, hsub:SUB, :] + upper],
                                     axis=1).reshape(grp, D_GROUP))
    o = jnp.concatenate(intra, axis=0) + jnp.concatenate(inter, axis=0)
    ms = _head_sum(o * o, km.bd16[...]) * (1.0 / HEAD_DIM)
    y = o * lax.rsqrt(ms + 1e-6) * nw * _silu(p_ref[:, c0 + 3 * D_GROUP:c0 + 4 * D_GROUP])
    o_ref[0, :, oc0:oc0 + D_GROUP] = y.astype(o_ref.dtype)


_HGRN_STAGES = 43


_C_RET, _C_CONF, _C_GDN, _C_HGRN, _C_END = 0, 4 * D_GROUP, 6 * D_GROUP, 12 * D_GROUP, 16 * D_GROUP
_VEC_ROWS = 16


class _HeadMasks:
    def __init__(self, bd16, bd_f, rope_bd16, rope_state_f):
        self.bd16 = bd16
        self.bd_f = bd_f
        self.rope_bd16 = rope_bd16
        self.rope_state_f = rope_state_f


def _project(x_ref, nmw_ref, w_ref, p_ref, tblk):
    for r0 in range(0, tblk, ROW_GROUP):
        x = x_ref[0, r0:r0 + ROW_GROUP, :]
        h = (x * lax.rsqrt(jnp.mean(x * x, axis=-1, keepdims=True) + 1e-6) * nmw_ref[...]).astype(BF16)
        for c0 in range(0, _C_END, PROJ_CHUNK):
            p_ref[r0:r0 + ROW_GROUP, c0:c0 + PROJ_CHUNK] = _dot(h, w_ref[:, c0:c0 + PROJ_CHUNK])


def _mix_kernel(x_ref, nmw_ref, w_ref, cos_ref, sin_ref, ccw_ref, gcw_ref, vec_ref, o_ref,
                p_ref, cbuf_ref, csh_ref, gbuf_ref, sret_ref, sgdn_ref, shg_ref,
                bd16_ref, bdf_ref, rbd16_ref, rsf_ref, *, tblk):
    @pl.when(pl.program_id(1) == 0)
    def _():
        r = _iota((D_GROUP, D_GROUP), 0)
        l = _iota((D_GROUP, D_GROUP), 1)
        rope_head = lambda i: (i % (D_GROUP // 2)) // (HEAD_DIM // 2)
        bd = jnp.where(r // HEAD_DIM == l // HEAD_DIM, 1.0, 0.0)
        bdf_ref[...] = bd
        bd16_ref[...] = bd.astype(BF16)
        rbd16_ref[...] = jnp.where(r // HEAD_DIM == rope_head(l), 1.0, 0.0).astype(BF16)
        rsf_ref[...] = jnp.where(rope_head(r) == l // HEAD_DIM, 1.0, 0.0)
        sret_ref[...] = jnp.zeros_like(sret_ref)
        sgdn_ref[...] = jnp.zeros_like(sgdn_ref)
        shg_ref[...] = jnp.zeros_like(shg_ref)
        cbuf_ref[0:CONF_PAD, :] = jnp.zeros((CONF_PAD, D_GROUP), F32)
        gbuf_ref[0:GDN_PAD, :] = jnp.zeros((GDN_PAD, 3 * D_GROUP), F32)

    km = _HeadMasks(bd16_ref, bdf_ref, rbd16_ref, rsf_ref)
    _project(x_ref, nmw_ref, w_ref, p_ref, tblk)
    vec = lambda i: vec_ref[i:i + 1, :]
    g = D_GROUP
    _interleave([
        (_ret_stages(p_ref, _C_RET, cos_ref, sin_ref, vec(0), o_ref, 0, sret_ref, km, tblk), _RET_STAGES),
        (_conf_stages(p_ref, _C_CONF, ccw_ref, vec(1), vec(2), vec(3), o_ref, g, cbuf_ref, csh_ref, tblk),
         _CONF_STAGES),
        (_gdn_stages(p_ref, _C_GDN, gcw_ref, vec(4), vec(5), vec(6), o_ref, 2 * g, gbuf_ref, sgdn_ref, km, tblk),
         _GDN_STAGES),
        (_hgrn_stages(p_ref, _C_HGRN, vec(7), vec(8), vec(9), vec(10), o_ref, 3 * g, shg_ref, km, tblk),
         _HGRN_STAGES),
    ])


def _mixers(x, nmw, w, cos, sin, ccw, gcw, vec):
    b, t, _ = x.shape
    tblk = min(MIX_TBLK, t)
    assert t % tblk == 0 and tblk % ROW_GROUP == 0
    const = lambda a: pl.BlockSpec(a.shape, lambda bi, ti: (0, 0), pipeline_mode=pl.Buffered(1))
    timed = lambda a: pl.BlockSpec((tblk, a.shape[1]), lambda bi, ti: (ti, 0))
    blk = pl.BlockSpec((1, tblk, D_MODEL), lambda bi, ti: (bi, ti, 0))
    return pl.pallas_call(
        functools.partial(_mix_kernel, tblk=tblk),
        name="mixers",
        grid=(b, t // tblk),
        in_specs=[blk, const(nmw), const(w), timed(cos), timed(sin), const(ccw), const(gcw), const(vec)],
        out_specs=blk,
        out_shape=jax.ShapeDtypeStruct((b, t, D_MODEL), BF16),
        scratch_shapes=[
            pltpu.VMEM((tblk, _C_END), F32),
            pltpu.VMEM((tblk + CONF_PAD, D_GROUP), F32),
            pltpu.VMEM((7, tblk + CONF_PAD, D_GROUP), F32),
            pltpu.VMEM((tblk + GDN_PAD, 3 * D_GROUP), F32),
            pltpu.VMEM((D_GROUP, D_GROUP), F32),
            pltpu.VMEM((D_GROUP, D_GROUP), F32),
            pltpu.VMEM((D_GROUP, D_GROUP), F32),
            pltpu.VMEM((D_GROUP, D_GROUP), BF16),
            pltpu.VMEM((D_GROUP, D_GROUP), F32),
            pltpu.VMEM((D_GROUP, D_GROUP), BF16),
            pltpu.VMEM((D_GROUP, D_GROUP), F32),
        ],
        compiler_params=pltpu.CompilerParams(
            dimension_semantics=("parallel", "arbitrary"), vmem_limit_bytes=VMEM_LIMIT),
    )(x, nmw, w, cos, sin, ccw, gcw, vec)


def _ffn_kernel(x_ref, m_ref, wo_ref, nfw_ref, wg_ref, wu_ref, wd_ref, nxt_ref, o_ref, *, last):
    y = x_ref[...] + _dot(m_ref[...], wo_ref[...])
    h = (y * lax.rsqrt(jnp.mean(y * y, axis=-1, keepdims=True) + 1e-6) * nfw_ref[...]).astype(BF16)
    down = None
    for f0, f1 in FF_CHUNKS:
        gate = _dot(h, wg_ref[:, f0:f1])
        up = _dot(h, wu_ref[:, f0:f1])
        part = _dot((_silu(gate) * up).astype(BF16), wd_ref[f0:f1, :])
        down = part if down is None else down + part
    y = y + down
    if last:
        y = y * lax.rsqrt(jnp.mean(y * y, axis=-1, keepdims=True) + 1e-6) * nxt_ref[...]
    o_ref[...] = y


def _out_ffn(x2d, mix, wo, nfw, wg, wu, wd, nxt_w, last):
    n = x2d.shape[0]
    tm = min(FFN_TM, n)
    row = pl.BlockSpec((tm, D_MODEL), lambda i: (i, 0))
    const = lambda shape: pl.BlockSpec(shape, lambda i: (0, 0), pipeline_mode=pl.Buffered(1))
    return pl.pallas_call(
        functools.partial(_ffn_kernel, last=last),
        name="out_ffn_last" if last else "out_ffn",
        grid=(n // tm,),
        in_specs=[row, row, const((D_MODEL, D_MODEL)), const((1, D_MODEL)), const((D_MODEL, D_FF)),
                  const((D_MODEL, D_FF)), const((D_FF, D_MODEL)), const((1, D_MODEL))],
        out_specs=row,
        out_shape=jax.ShapeDtypeStruct((n, D_MODEL), F32),
        compiler_params=pltpu.CompilerParams(
            dimension_semantics=("parallel",), vmem_limit_bytes=VMEM_LIMIT),
    )(x2d, mix, wo, nfw, wg, wu, wd, nxt_w)


def _rope_perm(w):
    d = w.shape[0]
    return w.reshape(d, N_HEADS, 2, HEAD_DIM // 2).transpose(0, 2, 1, 3).reshape(d, D_GROUP)


def _per_head_to_lanes(a):
    return jnp.repeat(a, HEAD_DIM, axis=-1)


def kernel(x, norm_mix_w, w_in, ret_norm_w, conf_conv_w, conf_conv_b, conf_ln_w, conf_ln_b, gdn_conv_w, gdn_A_log, gdn_dt_bias, gdn_norm_w, hgrn_lb_logits, hgrn_norm_w, w_out, norm_ffn_w, ffn_w_gate, ffn_w_up, ffn_w_down, final_norm_w):
    b, t, d = x.shape
    n = b * t
    g = D_GROUP
    row = lambda a: a.reshape(1, -1).astype(F32)

    pos = jnp.arange(t, dtype=F32)
    half = HEAD_DIM // 2
    inv = ROPE_BASE ** (-jnp.arange(half, dtype=F32) / half)
    ang = pos[:, None] * inv[None, :]
    cos = jnp.tile(jnp.cos(ang), (1, N_HEADS))
    sin = jnp.tile(jnp.sin(ang), (1, N_HEADS))

    lb_all = jnp.cumsum(jax.nn.softmax(hgrn_lb_logits.astype(F32), axis=0), axis=0)
    lb_all = lb_all - lb_all[0:1]

    for l in range(DEPTH):
        wl = w_in[l]
        o_gb = 10 * g
        w_all = jnp.concatenate([
            _rope_perm(wl[:, 0:g]), _rope_perm(wl[:, g:2 * g]), wl[:, 2 * g:4 * g],
            wl[:, 4 * g:6 * g],
            wl[:, 6 * g:10 * g], _per_head_to_lanes(wl[:, o_gb:o_gb + N_HEADS]),
            _per_head_to_lanes(wl[:, o_gb + N_HEADS:o_gb + 2 * N_HEADS]),
            wl[:, o_gb + 2 * N_HEADS:],
        ], axis=1).astype(BF16)
        lb = lb_all[l]
        vec = jnp.concatenate([
            row(ret_norm_w[l]), row(conf_conv_b[l]), row(conf_ln_w[l]), row(conf_ln_b[l]),
            row(_per_head_to_lanes(-jnp.exp(gdn_A_log[l].astype(F32)))),
            row(_per_head_to_lanes(gdn_dt_bias[l].astype(F32))),
            row(jnp.tile(gdn_norm_w[l], N_HEADS)),
            row(jnp.log(lb)), row(jnp.log1p(-lb)), row(1.0 - lb), row(jnp.tile(hgrn_norm_w[l], N_HEADS)),
            jnp.zeros((_VEC_ROWS - 11, g), F32)], axis=0)
        ccw = jnp.concatenate([conf_conv_w[l].astype(F32), jnp.zeros((CONF_PAD - CONF_KERNEL, g), F32)], axis=0)
        gcw = jnp.concatenate([gdn_conv_w[l].astype(F32), jnp.zeros((GDN_PAD - SHORT_CONV, 3 * g), F32)], axis=0)
        mix = _mixers(x, row(norm_mix_w[l]), w_all, cos, sin, ccw, gcw, vec)

        last = l == DEPTH - 1
        x = _out_ffn(x.reshape(n, d), mix.reshape(n, d), w_out[l].astype(BF16), row(norm_ffn_w[l]),
                     ffn_w_gate[l].astype(BF16), ffn_w_up[l].astype(BF16), ffn_w_down[l].astype(BF16),
                     row(final_norm_w), last).reshape(b, t, d)
    return x
```

```python
import functools

import numpy as np
import jax
import jax.numpy as jnp
from jax import lax
from jax.experimental import pallas as pl
from jax.experimental.pallas import tpu as pltpu

F32 = jnp.float32
BF16 = jnp.bfloat16

D_MODEL = 1024
DEPTH = 2
D_GROUP = 256
HEAD_DIM = 64
N_HEADS = 4
CHUNK = 64
SUB = 16
HGRN_STATE_CHUNKS = 2
CONF_KERNEL = 31
CONF_PAD = 32
SHORT_CONV = 4
GDN_PAD = 8
ROPE_BASE = 10000.0
D_FF = 2816
FF_CHUNKS = ((0, 1536), (1536, D_FF))
MIX_TBLK = 512
ROW_GROUP = 256
PROJ_CHUNK = 512
FFN_TM = 1024
VMEM_LIMIT = 60 * 1024 * 1024
LOG2E = 1.4426950408889634

_LOG_GAMMA = [float(np.log1p(-(2.0 ** (-5.0 - h)))) for h in range(N_HEADS)]


def _iota(shape, dim):
    return lax.broadcasted_iota(jnp.int32, shape, dim)


def _dot(a, b):
    return jnp.dot(a, b, preferred_element_type=F32)


def _dot_nt(a, b):
    return lax.dot_general(a, b, (((1,), (1,)), ((), ())), preferred_element_type=F32)


def _dot_tn(a, b):
    return lax.dot_general(a, b, (((0,), (0,)), ((), ())), preferred_element_type=F32)


def _sigmoid(x):
    return 1.0 / (1.0 + jnp.exp2(x * -LOG2E))


def _silu(x):
    return x * _sigmoid(x)


def _log1pexp(x):
    return jnp.log(1.0 + jnp.exp2(jnp.abs(x) * -LOG2E))


def _softplus(z):
    return jnp.maximum(z, 0.0) + _log1pexp(z)


def _dot_exact_lhs(m, x):
    hi = x.astype(BF16)
    lo = (x - hi.astype(F32)).astype(BF16)
    return _dot(m, hi) + _dot(m, lo)


def _head_sum(x, ones_bd16):
    return _dot(x.astype(BF16), ones_bd16)


def _stack_heads(x16, m16):
    return jnp.concatenate([x16] * N_HEADS, axis=0) * m16


def _select_by_head(hid, vals):
    out = jnp.full(hid.shape, vals[N_HEADS - 1], F32)
    for h in range(N_HEADS - 2, -1, -1):
        out = jnp.where(hid == h, vals[h], out)
    return out


def _interleave(gens):
    done = [0] * len(gens)
    live = list(range(len(gens)))
    while live:
        i = min(live, key=lambda j: (done[j] + 0.5) / gens[j][1])
        try:
            next(gens[i][0])
            done[i] += 1
        except StopIteration:
            live.remove(i)


def _ret_stages(p_ref, c0, cos_ref, sin_ref, gnw, o_ref, oc0, s_ref, km, tblk):
    c = CHUNK
    nc = tblk // c
    half = D_GROUP // 2
    lane = _iota((c, D_GROUP), 1)
    row = _iota((c, D_GROUP), 0)
    head_qk = (lane % half) // (HEAD_DIM // 2)
    head_v = lane // HEAD_DIM
    lg_qk = _select_by_head(head_qk, _LOG_GAMMA)
    lg_v = _select_by_head(head_v, _LOG_GAMMA)
    rowf = row.astype(F32)
    diff = row - (lane % c)
    decay_mask = jnp.where(diff >= 0, jnp.exp(jnp.maximum(diff, 0).astype(F32) * lg_v), 0.0)
    q_decay = jnp.exp((rowf + 1.0) * lg_qk)
    k_decay = jnp.exp((c - 1.0 - rowf) * lg_qk) * HEAD_DIM ** -0.5
    chunk_decay = jnp.exp(float(c) * lg_v[0:1, :])

    def rope(off, r0):
        x1 = p_ref[r0:r0 + c, c0 + off:c0 + off + half]
        x2 = p_ref[r0:r0 + c, c0 + off + half:c0 + off + D_GROUP]
        cos = cos_ref[r0:r0 + c, :]
        sin = sin_ref[r0:r0 + c, :]
        return jnp.concatenate([x1 * cos - x2 * sin, x1 * sin + x2 * cos], axis=1)

    rows = [ci * c for ci in range(nc)]
    q = [rope(0, r0) for r0 in rows]
    k = [rope(D_GROUP, r0) for r0 in rows]
    v16 = [p_ref[r0:r0 + c, c0 + 2 * D_GROUP:c0 + 3 * D_GROUP].astype(BF16) for r0 in rows]
    yield
    kst = [_stack_heads((k[i] * HEAD_DIM ** -0.5).astype(BF16), km.rope_bd16[...]) for i in range(nc)]
    scores = [_dot_nt(q[i].astype(BF16), kst[i]) * decay_mask for i in range(nc)]
    yield
    intra = [_dot(scores[i].astype(BF16), _stack_heads(v16[i], km.bd16[...])) for i in range(nc)]
    kv = [_dot_tn((k[i] * k_decay).astype(BF16), v16[i]) * km.rope_state_f[...] for i in range(nc)]
    qd = [(q[i] * q_decay).astype(BF16) for i in range(nc)]
    yield
    s = s_ref[...]
    outs = []
    for i in range(nc):
        outs.append(intra[i] + _dot(qd[i], s.astype(BF16)))
        s = s * chunk_decay + kv[i]
    s_ref[...] = s
    yield
    o = jnp.concatenate(outs, axis=0)
    mu = _head_sum(o, km.bd16[...]) * (1.0 / HEAD_DIM)
    d = o - mu
    var = _head_sum(d * d, km.bd16[...]) * (1.0 / HEAD_DIM)
    y = d * lax.rsqrt(var + 1e-5) * gnw * _silu(p_ref[:, c0 + 3 * D_GROUP:c0 + 4 * D_GROUP])
    o_ref[0, :, oc0:oc0 + D_GROUP] = y.astype(o_ref.dtype)


_RET_STAGES = 5


def _conf_stages(p_ref, c0, cw_ref, cb, lnw, lnb, o_ref, oc0, buf_ref, sh_ref, tblk):
    cw = cw_ref[...]
    grp = ROW_GROUP
    first = CONF_PAD - (CONF_KERNEL - 1)
    taps_of = [[tap for tap in range(CONF_KERNEL) if (first + tap) % 8 == r] for r in range(8)]
    for r0 in range(0, tblk, grp):
        buf_ref[CONF_PAD + r0:CONF_PAD + r0 + grp, :] = (
            p_ref[r0:r0 + grp, c0:c0 + D_GROUP] * _sigmoid(p_ref[r0:r0 + grp, c0 + D_GROUP:c0 + 2 * D_GROUP]))
        yield
        for r in range(1, 8):
            n_r = first + taps_of[r][-1] - r + grp
            sh_ref[r - 1, r0:r0 + n_r, :] = buf_ref[r0 + r:r0 + r + n_r, :]
        yield
        acc = None
        for r in range(8):
            for tap in taps_of[r]:
                a0 = r0 + first + tap - r
                src = buf_ref[a0:a0 + grp, :] if r == 0 else sh_ref[r - 1, a0:a0 + grp, :]
                term = cw[tap:tap + 1, :] * src
                acc = term if acc is None else acc + term
            if r % 2 == 1:
                yield
        y = acc + cb
        mu = jnp.mean(y, axis=-1, keepdims=True)
        d = y - mu
        var = jnp.mean(d * d, axis=-1, keepdims=True)
        z = d * lax.rsqrt(var + 1e-5) * lnw + lnb
        o_ref[0, r0:r0 + grp, oc0:oc0 + D_GROUP] = _silu(z).astype(o_ref.dtype)
        yield
    buf_ref[0:CONF_PAD, :] = buf_ref[tblk:tblk + CONF_PAD, :]


_CONF_STAGES = 15


def _gdn_stages(p_ref, c0, cw_ref, nega, dtb, nw, o_ref, oc0, buf_ref, s_ref, km, tblk):
    cw = cw_ref[...]
    c = CHUNK
    nc = tblk // c
    grp = ROW_GROUP
    first = GDN_PAD - (SHORT_CONV - 1)

    lane = _iota((c, D_GROUP), 1)
    row = _iota((c, D_GROUP), 0)
    col_j = lane % c
    eye_f = jnp.where(row == col_j, 1.0, 0.0)
    incl_f = jnp.where(row >= col_j, 1.0, 0.0)
    strict_f = incl_f - eye_f
    rr = _iota((grp, grp), 0)
    cc = _iota((grp, grp), 1)
    block_tri = jnp.where((rr // c == cc // c) & (rr >= cc), 1.0, 0.0).astype(BF16)

    def bd(x):
        return _stack_heads(x.astype(BF16), km.bd16[...])

    Gc, kc, k16, vb16, kbe16, qe, lhs_aa = [], [], [], [], [], [], []
    for r0 in range(0, tblk, grp):
        buf_ref[GDN_PAD + r0:GDN_PAD + r0 + grp, :] = p_ref[r0:r0 + grp, c0:c0 + 3 * D_GROUP]
        beta = _sigmoid(p_ref[r0:r0 + grp, c0 + 4 * D_GROUP:c0 + 5 * D_GROUP])
        g = nega * _softplus(p_ref[r0:r0 + grp, c0 + 5 * D_GROUP:c0 + 6 * D_GROUP] + dtb)
        acc = None
        for tap in range(SHORT_CONV):
            term = cw[tap:tap + 1, :] * buf_ref[r0 + first + tap:r0 + first + tap + grp, :]
            acc = term if acc is None else acc + term
        qkv = _silu(acc)
        q = qkv[:, 0:D_GROUP]
        k = qkv[:, D_GROUP:2 * D_GROUP]
        v = qkv[:, 2 * D_GROUP:3 * D_GROUP]
        yield
        q = q * lax.rsqrt(_head_sum(q * q, km.bd16[...]) + 1e-6) * HEAD_DIM ** -0.5
        k = k * lax.rsqrt(_head_sum(k * k, km.bd16[...]) + 1e-6)
        G = _dot_exact_lhs(block_tri, g)
        eG = jnp.exp(G)
        kb = k * beta
        vb = (v * beta).astype(BF16)
        kbe = (kb * eG).astype(BF16)
        qeg = q * eG
        kg16 = k.astype(BF16)
        for j in range(0, grp, c):
            Gc.append(G[j:j + c])
            kc.append(k[j:j + c])
            k16.append(kg16[j:j + c])
            vb16.append(vb[j:j + c])
            kbe16.append(kbe[j:j + c])
            qe.append(qeg[j:j + c])
            lhs_aa.append(jnp.concatenate([kb[j:j + c], q[j:j + c]], axis=0).astype(BF16))
        yield
    buf_ref[0:GDN_PAD, :] = buf_ref[tblk:tblk + GDN_PAD, :]

    g_last = [Gc[i][c - 1:c, :] for i in range(nc)]
    L = []
    for i in range(nc):
        g_row = jnp.sum(eye_f * Gc[i], axis=0, keepdims=True)
        L.append(jnp.exp(jnp.minimum(Gc[i] - g_row, 0.0)))
    yield
    aa = [_dot_nt(lhs_aa[i], bd(k16[i])) for i in range(nc)]
    A = [aa[i][0:c] * (L[i] * strict_f) for i in range(nc)]
    attn = [(aa[i][c:2 * c] * (L[i] * incl_f)).astype(BF16) for i in range(nc)]
    yield
    T = [eye_f - A[i] for i in range(nc)]
    P = [_dot(A[i].astype(BF16), bd(A[i])) for i in range(nc)]
    yield
    for it in range(5):
        if it < 4:
            tp = [_dot(jnp.concatenate([T[i], P[i]], axis=0).astype(BF16), bd(P[i])) for i in range(nc)]
            T = [T[i] + tp[i][0:c] for i in range(nc)]
            P = [tp[i][c:2 * c] for i in range(nc)]
        else:
            T = [T[i] + _dot(T[i].astype(BF16), bd(P[i])) for i in range(nc)]
        yield
    T16 = [T[i].astype(BF16) for i in range(nc)]
    u = [_dot(T16[i], bd(vb16[i])) for i in range(nc)]
    w = [_dot(T16[i], bd(kbe16[i])) for i in range(nc)]
    lhs_ws = [jnp.concatenate([w[i], qe[i]], axis=0).astype(BF16) for i in range(nc)]
    k_dec = [(kc[i] * jnp.exp(g_last[i] - Gc[i])).astype(BF16) for i in range(nc)]
    e_last = [jnp.exp(g_last[i]) for i in range(nc)]
    yield
    s = s_ref[...]
    outs = []
    for i in range(nc):
        ws = _dot(lhs_ws[i], s.astype(BF16))
        v_new = u[i] - ws[0:c]
        outs.append(ws[c:2 * c] + _dot(attn[i], bd(v_new)))
        s = s * e_last[i] + _dot_tn(k_dec[i], v_new.astype(BF16)) * km.bd_f[...]
        yield
    s_ref[...] = s
    o = jnp.concatenate(outs, axis=0)
    ms = _head_sum(o * o, km.bd16[...]) * (1.0 / HEAD_DIM)
    y = o * lax.rsqrt(ms + 1e-6) * nw * _silu(p_ref[:, c0 + 3 * D_GROUP:c0 + 4 * D_GROUP])
    o_ref[0, :, oc0:oc0 + D_GROUP] = y.astype(o_ref.dtype)


_GDN_STAGES = 22


def _hgrn_stages(p_ref, c0, log_lb, l1m, oml, nw, o_ref, oc0, st_ref, km, tblk, zero_lb):
    grp = ROW_GROUP
    nsub = grp // SUB
    hsub = SUB // 2
    rr = _iota((grp, grp), 0)
    cc = _iota((grp, grp), 1)
    block_tri = jnp.where((rr // SUB == cc // SUB) & (rr >= cc), 1.0, 0.0).astype(BF16)
    row_in_sub = _iota((nsub, SUB, D_GROUP), 1)
    row_in_half = _iota((nsub, hsub, D_GROUP), 1) + hsub

    bd16_sub = jnp.concatenate([km.bd16[h * HEAD_DIM:h * HEAD_DIM + SUB, :] for h in range(N_HEADS)], axis=0)

    def stack_sub(x16):
        return jnp.concatenate([x16] * N_HEADS, axis=0) * bd16_sub

    groups, lhs_all, kv_all, e_all, cross_all = [], [], [], [], []
    for r0 in range(0, tblk, grp):
        q = p_ref[r0:r0 + grp, c0:c0 + D_GROUP]
        x = p_ref[r0:r0 + grp, c0 + D_GROUP:c0 + 2 * D_GROUP]
        v = p_ref[r0:r0 + grp, c0 + 2 * D_GROUP:c0 + 3 * D_GROUP]
        log_sig = jnp.minimum(x, 0.0) - _log1pexp(x)
        if zero_lb:
            log_f = log_sig
            k = _sigmoid(-x)
        else:
            b_ = l1m + log_sig
            log_f = jnp.maximum(log_lb, b_) + _log1pexp(log_lb - b_)
            k = oml * _sigmoid(-x)
        G = _dot_exact_lhs(block_tri, log_f)
        G3 = G.reshape(nsub, SUB, D_GROUP)
        k3 = k.reshape(nsub, SUB, D_GROUP)
        groups.append((G3 * LOG2E, q.reshape(nsub, SUB, D_GROUP), k3, v.reshape(nsub, SUB, D_GROUP)))
        eG = jnp.exp(G)
        qg = q * eG
        k_end = (k3 * jnp.exp(G3[:, SUB - 1:SUB, :] - G3)).reshape(grp, D_GROUP)
        v16 = v.astype(BF16)
        ns = HGRN_STATE_CHUNKS
        for lo in range(0, grp, ns * SUB):
            rows = [slice(lo + i * SUB, lo + (i + 1) * SUB) for i in range(ns)]
            e_c = [eG[r.stop - 1:r.stop, :] for r in rows]
            v_st = [stack_sub(v16[r]) for r in rows[:-1]]
            lhs, cross = [qg[rows[0]]], [None]
            before = None
            for i in range(1, ns):
                before = e_c[i - 1] if before is None else before * e_c[i - 1]
                lhs.append(qg[rows[i]] * before)
                k_st, between = [], None
                for j in range(i - 1, -1, -1):
                    kj = k_end[rows[j]] if between is None else k_end[rows[j]] * between
                    k_st.insert(0, stack_sub(kj.astype(BF16)))
                    between = e_c[j] if between is None else between * e_c[j]
                sc = _dot_nt(qg[rows[i]].astype(BF16), jnp.concatenate(k_st, axis=0))
                cross.append(_dot(sc.astype(BF16), jnp.concatenate(v_st[:i], axis=0)))
            k_all, after = [k_end[rows[ns - 1]]], None
            for i in range(ns - 2, -1, -1):
                after = e_c[i + 1] if after is None else after * e_c[i + 1]
                k_all.insert(0, k_end[rows[i]] * after)
            lhs_all.append(jnp.concatenate(lhs, axis=0).astype(BF16))
            cross_all.append(cross)
            kv_all.append(_dot_tn(v16[lo:lo + ns * SUB], jnp.concatenate(k_all, axis=0).astype(BF16))
                          * km.bd_f[...])
            e_all.append(before * e_c[ns - 1])
        yield

    st = st_ref[...]
    inter = []
    for bi in range(len(kv_all)):
        from_state = _dot_nt(lhs_all[bi], st.astype(BF16))
        for i, cross in enumerate(cross_all[bi]):
            part = from_state[i * SUB:(i + 1) * SUB]
            inter.append(part if cross is None else part + cross)
        st = st * e_all[bi] + kv_all[bi]
        if (bi + 1) % max(1, len(kv_all) // 8) == 0:
            yield
    st_ref[...] = st

    intra = []
    for G2, q3, k3, v3 in groups:
        full = None
        for j in range(hsub):
            e = jnp.exp2(jnp.where(row_in_sub >= j, G2 - G2[:, j:j + 1, :], -1e30))
            z = q3 * e * k3[:, j:j + 1, :]
            sc = _head_sum(z.reshape(grp, D_GROUP), km.bd16[...])
            term = sc.reshape(nsub, SUB, D_GROUP) * v3[:, j:j + 1, :]
            full = term if full is None else full + term
            yield
        Gh = G2[:, hsub:SUB, :]
        qh = q3[:, hsub:SUB, :]
        upper = None
        for j in range(hsub, SUB):
            e = jnp.exp2(jnp.where(row_in_half >= j, Gh - G2[:, j:j + 1, :], -1e30))
            z = qh * e * k3[:, j:j + 1, :]
            sc = _head_sum(z.reshape(grp // 2, D_GROUP), km.bd16[...])
            term = sc.reshape(nsub, hsub, D_GROUP) * v3[:, j:j + 1, :]
            upper = term if upper is None else upper + term
            yield
        intra.append(jnp.concatenate([full[:, 0:hsub, :], full[:, hsub:SUB, :] + upper],
                                     axis=1).reshape(grp, D_GROUP))
    o = jnp.concatenate(intra, axis=0) + jnp.concatenate(inter, axis=0)
    ms = _head_sum(o * o, km.bd16[...]) * (1.0 / HEAD_DIM)
    y = o * lax.rsqrt(ms + 1e-6) * nw * _silu(p_ref[:, c0 + 3 * D_GROUP:c0 + 4 * D_GROUP])
    o_ref[0, :, oc0:oc0 + D_GROUP] = y.astype(o_ref.dtype)


_HGRN_STAGES = 43


_C_RET, _C_CONF, _C_GDN, _C_HGRN, _C_END = 0, 4 * D_GROUP, 6 * D_GROUP, 12 * D_GROUP, 16 * D_GROUP
_VEC_ROWS = 16


class _HeadMasks:
    def __init__(self, bd16, bd_f, rope_bd16, rope_state_f):
        self.bd16 = bd16
        self.bd_f = bd_f
        self.rope_bd16 = rope_bd16
        self.rope_state_f = rope_state_f


def _project(x_ref, nmw_ref, w_ref, p_ref, tblk):
    for r0 in range(0, tblk, ROW_GROUP):
        x = x_ref[0, r0:r0 + ROW_GROUP, :]
        h = (x * lax.rsqrt(jnp.mean(x * x, axis=-1, keepdims=True) + 1e-6) * nmw_ref[...]).astype(BF16)
        for c0 in range(0, _C_END, PROJ_CHUNK):
            p_ref[r0:r0 + ROW_GROUP, c0:c0 + PROJ_CHUNK] = _dot(h, w_ref[:, c0:c0 + PROJ_CHUNK])


def _mix_kernel(x_ref, nmw_ref, w_ref, cos_ref, sin_ref, ccw_ref, gcw_ref, vec_ref, o_ref,
                p_ref, cbuf_ref, csh_ref, gbuf_ref, sret_ref, sgdn_ref, shg_ref,
                bd16_ref, bdf_ref, rbd16_ref, rsf_ref, *, tblk, zero_lb):
    @pl.when(pl.program_id(1) == 0)
    def _():
        r = _iota((D_GROUP, D_GROUP), 0)
        l = _iota((D_GROUP, D_GROUP), 1)
        rope_head = lambda i: (i % (D_GROUP // 2)) // (HEAD_DIM // 2)
        bd = jnp.where(r // HEAD_DIM == l // HEAD_DIM, 1.0, 0.0)
        bdf_ref[...] = bd
        bd16_ref[...] = bd.astype(BF16)
        rbd16_ref[...] = jnp.where(r // HEAD_DIM == rope_head(l), 1.0, 0.0).astype(BF16)
        rsf_ref[...] = jnp.where(rope_head(r) == l // HEAD_DIM, 1.0, 0.0)
        sret_ref[...] = jnp.zeros_like(sret_ref)
        sgdn_ref[...] = jnp.zeros_like(sgdn_ref)
        shg_ref[...] = jnp.zeros_like(shg_ref)
        cbuf_ref[0:CONF_PAD, :] = jnp.zeros((CONF_PAD, D_GROUP), F32)
        gbuf_ref[0:GDN_PAD, :] = jnp.zeros((GDN_PAD, 3 * D_GROUP), F32)

    km = _HeadMasks(bd16_ref, bdf_ref, rbd16_ref, rsf_ref)
    _project(x_ref, nmw_ref, w_ref, p_ref, tblk)
    vec = lambda i: vec_ref[i:i + 1, :]
    g = D_GROUP
    _interleave([
        (_ret_stages(p_ref, _C_RET, cos_ref, sin_ref, vec(0), o_ref, 0, sret_ref, km, tblk), _RET_STAGES),
        (_conf_stages(p_ref, _C_CONF, ccw_ref, vec(1), vec(2), vec(3), o_ref, g, cbuf_ref, csh_ref, tblk),
         _CONF_STAGES),
        (_gdn_stages(p_ref, _C_GDN, gcw_ref, vec(4), vec(5), vec(6), o_ref, 2 * g, gbuf_ref, sgdn_ref, km, tblk),
         _GDN_STAGES),
        (_hgrn_stages(p_ref, _C_HGRN, vec(7), vec(8), vec(9), vec(10), o_ref, 3 * g, shg_ref, km, tblk, zero_lb),
         _HGRN_STAGES),
    ])


def _mixers(x, nmw, w, cos, sin, ccw, gcw, vec, zero_lb):
    b, t, _ = x.shape
    tblk = min(MIX_TBLK, t)
    assert t % tblk == 0 and tblk % ROW_GROUP == 0
    const = lambda a: pl.BlockSpec(a.shape, lambda bi, ti: (0, 0), pipeline_mode=pl.Buffered(1))
    timed = lambda a: pl.BlockSpec((tblk, a.shape[1]), lambda bi, ti: (ti, 0))
    blk = pl.BlockSpec((1, tblk, D_MODEL), lambda bi, ti: (bi, ti, 0))
    return pl.pallas_call(
        functools.partial(_mix_kernel, tblk=tblk, zero_lb=zero_lb),
        name="mixers",
        grid=(b, t // tblk),
        in_specs=[blk, const(nmw), const(w), timed(cos), timed(sin), const(ccw), const(gcw), const(vec)],
        out_specs=blk,
        out_shape=jax.ShapeDtypeStruct((b, t, D_MODEL), BF16),
        scratch_shapes=[
            pltpu.VMEM((tblk, _C_END), F32),
            pltpu.VMEM((tblk + CONF_PAD, D_GROUP), F32),
            pltpu.VMEM((7, tblk + CONF_PAD, D_GROUP), F32),
            pltpu.VMEM((tblk + GDN_PAD, 3 * D_GROUP), F32),
            pltpu.VMEM((D_GROUP, D_GROUP), F32),
            pltpu.VMEM((D_GROUP, D_GROUP), F32),
            pltpu.VMEM((D_GROUP, D_GROUP), F32),
            pltpu.VMEM((D_GROUP, D_GROUP), BF16),
            pltpu.VMEM((D_GROUP, D_GROUP), F32),
            pltpu.VMEM((D_GROUP, D_GROUP), BF16),
            pltpu.VMEM((D_GROUP, D_GROUP), F32),
        ],
        compiler_params=pltpu.CompilerParams(
            dimension_semantics=("parallel", "arbitrary"), vmem_limit_bytes=VMEM_LIMIT),
    )(x, nmw, w, cos, sin, ccw, gcw, vec)


def _ffn_kernel(x_ref, m_ref, wo_ref, nfw_ref, wg_ref, wu_ref, wd_ref, nxt_ref, o_ref, *, last):
    y = x_ref[...] + _dot(m_ref[...], wo_ref[...])
    h = (y * lax.rsqrt(jnp.mean(y * y, axis=-1, keepdims=True) + 1e-6) * nfw_ref[...]).astype(BF16)
    down = None
    for f0, f1 in FF_CHUNKS:
        gate = _dot(h, wg_ref[:, f0:f1])
        up = _dot(h, wu_ref[:, f0:f1])
        part = _dot((_silu(gate) * up).astype(BF16), wd_ref[f0:f1, :])
        down = part if down is None else down + part
    y = y + down
    if last:
        y = y * lax.rsqrt(jnp.mean(y * y, axis=-1, keepdims=True) + 1e-6) * nxt_ref[...]
    o_ref[...] = y


def _out_ffn(x2d, mix, wo, nfw, wg, wu, wd, nxt_w, last):
    n = x2d.shape[0]
    tm = min(FFN_TM, n)
    row = pl.BlockSpec((tm, D_MODEL), lambda i: (i, 0))
    const = lambda shape: pl.BlockSpec(shape, lambda i: (0, 0), pipeline_mode=pl.Buffered(1))
    return pl.pallas_call(
        functools.partial(_ffn_kernel, last=last),
        name="out_ffn_last" if last else "out_ffn",
        grid=(n // tm,),
        in_specs=[row, row, const((D_MODEL, D_MODEL)), const((1, D_MODEL)), const((D_MODEL, D_FF)),
                  const((D_MODEL, D_FF)), const((D_FF, D_MODEL)), const((1, D_MODEL))],
        out_specs=row,
        out_shape=jax.ShapeDtypeStruct((n, D_MODEL), F32),
        compiler_params=pltpu.CompilerParams(
            dimension_semantics=("parallel",), vmem_limit_bytes=VMEM_LIMIT),
    )(x2d, mix, wo, nfw, wg, wu, wd, nxt_w)


def _rope_perm(w):
    d = w.shape[0]
    return w.reshape(d, N_HEADS, 2, HEAD_DIM // 2).transpose(0, 2, 1, 3).reshape(d, D_GROUP)


def _per_head_to_lanes(a):
    return jnp.repeat(a, HEAD_DIM, axis=-1)


def kernel(x, norm_mix_w, w_in, ret_norm_w, conf_conv_w, conf_conv_b, conf_ln_w, conf_ln_b, gdn_conv_w, gdn_A_log, gdn_dt_bias, gdn_norm_w, hgrn_lb_logits, hgrn_norm_w, w_out, norm_ffn_w, ffn_w_gate, ffn_w_up, ffn_w_down, final_norm_w):
    b, t, d = x.shape
    n = b * t
    g = D_GROUP
    row = lambda a: a.reshape(1, -1).astype(F32)

    pos = jnp.arange(t, dtype=F32)
    half = HEAD_DIM // 2
    inv = ROPE_BASE ** (-jnp.arange(half, dtype=F32) / half)
    ang = pos[:, None] * inv[None, :]
    cos = jnp.tile(jnp.cos(ang), (1, N_HEADS))
    sin = jnp.tile(jnp.sin(ang), (1, N_HEADS))

    lb_all = jnp.cumsum(jax.nn.softmax(hgrn_lb_logits.astype(F32), axis=0), axis=0)
    lb_all = lb_all - lb_all[0:1]

    for l in range(DEPTH):
        wl = w_in[l]
        o_gb = 10 * g
        w_all = jnp.concatenate([
            _rope_perm(wl[:, 0:g]), _rope_perm(wl[:, g:2 * g]), wl[:, 2 * g:4 * g],
            wl[:, 4 * g:6 * g],
            wl[:, 6 * g:10 * g], _per_head_to_lanes(wl[:, o_gb:o_gb + N_HEADS]),
            _per_head_to_lanes(wl[:, o_gb + N_HEADS:o_gb + 2 * N_HEADS]),
            wl[:, o_gb + 2 * N_HEADS:],
        ], axis=1).astype(BF16)
        lb = lb_all[l]
        vec = jnp.concatenate([
            row(ret_norm_w[l]), row(conf_conv_b[l]), row(conf_ln_w[l]), row(conf_ln_b[l]),
            row(_per_head_to_lanes(-jnp.exp(gdn_A_log[l].astype(F32)))),
            row(_per_head_to_lanes(gdn_dt_bias[l].astype(F32))),
            row(jnp.tile(gdn_norm_w[l], N_HEADS)),
            row(jnp.log(lb)), row(jnp.log1p(-lb)), row(1.0 - lb), row(jnp.tile(hgrn_norm_w[l], N_HEADS)),
            jnp.zeros((_VEC_ROWS - 11, g), F32)], axis=0)
        ccw = jnp.concatenate([conf_conv_w[l].astype(F32), jnp.zeros((CONF_PAD - CONF_KERNEL, g), F32)], axis=0)
        gcw = jnp.concatenate([gdn_conv_w[l].astype(F32), jnp.zeros((GDN_PAD - SHORT_CONV, 3 * g), F32)], axis=0)
        mix = _mixers(x, row(norm_mix_w[l]), w_all, cos, sin, ccw, gcw, vec, zero_lb=(l == 0))

        last = l == DEPTH - 1
        x = _out_ffn(x.reshape(n, d), mix.reshape(n, d), w_out[l].astype(BF16), row(norm_ffn_w[l]),
                     ffn_w_gate[l].astype(BF16), ffn_w_up[l].astype(BF16), ffn_w_down[l].astype(BF16),
                     row(final_norm_w), last).reshape(b, t, d)
    return x
```

```python
import functools

import numpy as np
import jax
import jax.numpy as jnp
from jax import lax
from jax.experimental import pallas as pl
from jax.experimental.pallas import tpu as pltpu

F32 = jnp.float32
BF16 = jnp.bfloat16

D_MODEL = 1024
DEPTH = 2
D_GROUP = 256
HEAD_DIM = 64
N_HEADS = 4
CHUNK = 64
SUB = 16
HGRN_STATE_CHUNKS = 2
CONF_KERNEL = 31
CONF_PAD = 32
SHORT_CONV = 4
GDN_PAD = 8
ROPE_BASE = 10000.0
D_FF = 2816
FF_CHUNKS = ((0, 1536), (1536, D_FF))
MIX_TBLK = 512
ROW_GROUP = 256
PROJ_CHUNK = 512
FFN_TM = 1024
VMEM_LIMIT = 60 * 1024 * 1024
LOG2E = 1.4426950408889634

_LOG_GAMMA = [float(np.log1p(-(2.0 ** (-5.0 - h)))) for h in range(N_HEADS)]


def _iota(shape, dim):
    return lax.broadcasted_iota(jnp.int32, shape, dim)


def _dot(a, b):
    return jnp.dot(a, b, preferred_element_type=F32)


def _dot_nt(a, b):
    return lax.dot_general(a, b, (((1,), (1,)), ((), ())), preferred_element_type=F32)


def _dot_tn(a, b):
    return lax.dot_general(a, b, (((0,), (0,)), ((), ())), preferred_element_type=F32)


def _sigmoid(x):
    return 1.0 / (1.0 + jnp.exp2(x * -LOG2E))


def _silu(x):
    return x * _sigmoid(x)


def _log1pexp(x):
    return jnp.log(1.0 + jnp.exp2(jnp.abs(x) * -LOG2E))


def _softplus(z):
    return jnp.maximum(z, 0.0) + _log1pexp(z)


def _dot_exact_lhs(m, x):
    hi = x.astype(BF16)
    lo = (x - hi.astype(F32)).astype(BF16)
    return _dot(m, hi) + _dot(m, lo)


def _head_sum(x, ones_bd16):
    return _dot(x.astype(BF16), ones_bd16)


def _stack_heads(x16, m16):
    return jnp.concatenate([x16] * N_HEADS, axis=0) * m16


def _select_by_head(hid, vals):
    out = jnp.full(hid.shape, vals[N_HEADS - 1], F32)
    for h in range(N_HEADS - 2, -1, -1):
        out = jnp.where(hid == h, vals[h], out)
    return out


def _interleave(gens):
    done = [0] * len(gens)
    live = list(range(len(gens)))
    while live:
        i = min(live, key=lambda j: (done[j] + 0.5) / gens[j][1])
        try:
            next(gens[i][0])
            done[i] += 1
        except StopIteration:
            live.remove(i)


def _ret_stages(p_ref, c0, cos_ref, sin_ref, gnw, o_ref, oc0, s_ref, km, tblk):
    c = CHUNK
    nc = tblk // c
    half = D_GROUP // 2
    lane = _iota((c, D_GROUP), 1)
    row = _iota((c, D_GROUP), 0)
    head_qk = (lane % half) // (HEAD_DIM // 2)
    head_v = lane // HEAD_DIM
    lg_qk = _select_by_head(head_qk, _LOG_GAMMA)
    lg_v = _select_by_head(head_v, _LOG_GAMMA)
    rowf = row.astype(F32)
    diff = row - (lane % c)
    decay_mask = jnp.where(diff >= 0, jnp.exp(jnp.maximum(diff, 0).astype(F32) * lg_v), 0.0)
    q_decay = jnp.exp((rowf + 1.0) * lg_qk)
    k_decay = jnp.exp((c - 1.0 - rowf) * lg_qk) * HEAD_DIM ** -0.5
    chunk_decay = jnp.exp(float(c) * lg_v[0:1, :])

    def rope(off, r0):
        x1 = p_ref[r0:r0 + c, c0 + off:c0 + off + half]
        x2 = p_ref[r0:r0 + c, c0 + off + half:c0 + off + D_GROUP]
        cos = cos_ref[r0:r0 + c, :]
        sin = sin_ref[r0:r0 + c, :]
        return jnp.concatenate([x1 * cos - x2 * sin, x1 * sin + x2 * cos], axis=1)

    rows = [ci * c for ci in range(nc)]
    q = [rope(0, r0) for r0 in rows]
    k = [rope(D_GROUP, r0) for r0 in rows]
    v16 = [p_ref[r0:r0 + c, c0 + 2 * D_GROUP:c0 + 3 * D_GROUP].astype(BF16) for r0 in rows]
    yield
    kst = [_stack_heads((k[i] * HEAD_DIM ** -0.5).astype(BF16), km.rope_bd16[...]) for i in range(nc)]
    scores = [_dot_nt(q[i].astype(BF16), kst[i]) * decay_mask for i in range(nc)]
    yield
    intra = [_dot(scores[i].astype(BF16), _stack_heads(v16[i], km.bd16[...])) for i in range(nc)]
    kv = [_dot_tn((k[i] * k_decay).astype(BF16), v16[i]) * km.rope_state_f[...] for i in range(nc)]
    qd = [(q[i] * q_decay).astype(BF16) for i in range(nc)]
    yield
    s = s_ref[...]
    outs = []
    for i in range(nc):
        outs.append(intra[i] + _dot(qd[i], s.astype(BF16)))
        s = s * chunk_decay + kv[i]
    s_ref[...] = s
    yield
    o = jnp.concatenate(outs, axis=0)
    mu = _head_sum(o, km.bd16[...]) * (1.0 / HEAD_DIM)
    d = o - mu
    var = _head_sum(d * d, km.bd16[...]) * (1.0 / HEAD_DIM)
    y = d * lax.rsqrt(var + 1e-5) * gnw * _silu(p_ref[:, c0 + 3 * D_GROUP:c0 + 4 * D_GROUP])
    o_ref[0, :, oc0:oc0 + D_GROUP] = y.astype(o_ref.dtype)


_RET_STAGES = 5


def _conf_stages(p_ref, c0, cw_ref, cb, lnw, lnb, o_ref, oc0, buf_ref, sh_ref, tblk):
    cw = cw_ref[...]
    grp = ROW_GROUP
    first = CONF_PAD - (CONF_KERNEL - 1)
    taps_of = [[tap for tap in range(CONF_KERNEL) if (first + tap) % 8 == r] for r in range(8)]
    for r0 in range(0, tblk, grp):
        buf_ref[CONF_PAD + r0:CONF_PAD + r0 + grp, :] = (
            p_ref[r0:r0 + grp, c0:c0 + D_GROUP] * _sigmoid(p_ref[r0:r0 + grp, c0 + D_GROUP:c0 + 2 * D_GROUP]))
        yield
        for r in range(1, 8):
            n_r = first + taps_of[r][-1] - r + grp
            sh_ref[r - 1, r0:r0 + n_r, :] = buf_ref[r0 + r:r0 + r + n_r, :]
        yield
        acc = None
        for r in range(8):
            for tap in taps_of[r]:
                a0 = r0 + first + tap - r
                src = buf_ref[a0:a0 + grp, :] if r == 0 else sh_ref[r - 1, a0:a0 + grp, :]
                term = cw[tap:tap + 1, :] * src
                acc = term if acc is None else acc + term
            if r % 2 == 1:
                yield
        y = acc + cb
        mu = jnp.mean(y, axis=-1, keepdims=True)
        d = y - mu
        var = jnp.mean(d * d, axis=-1, keepdims=True)
        z = d * lax.rsqrt(var + 1e-5) * lnw + lnb
        o_ref[0, r0:r0 + grp, oc0:oc0 + D_GROUP] = _silu(z).astype(o_ref.dtype)
        yield
    buf_ref[0:CONF_PAD, :] = buf_ref[tblk:tblk + CONF_PAD, :]


_CONF_STAGES = 15


def _gdn_stages(p_ref, c0, cw_ref, nega, dtb, nw, o_ref, oc0, buf_ref, s_ref, km, tblk):
    cw = cw_ref[...]
    c = CHUNK
    nc = tblk // c
    grp = ROW_GROUP
    first = GDN_PAD - (SHORT_CONV - 1)

    lane = _iota((c, D_GROUP), 1)
    row = _iota((c, D_GROUP), 0)
    col_j = lane % c
    eye_f = jnp.where(row == col_j, 1.0, 0.0)
    incl_f = jnp.where(row >= col_j, 1.0, 0.0)
    strict_f = incl_f - eye_f
    rr = _iota((grp, grp), 0)
    cc = _iota((grp, grp), 1)
    block_tri = jnp.where((rr // c == cc // c) & (rr >= cc), 1.0, 0.0).astype(BF16)

    def bd(x):
        return _stack_heads(x.astype(BF16), km.bd16[...])

    Gc, kc, k16, vb16, kbe16, qe, lhs_aa = [], [], [], [], [], [], []
    for r0 in range(0, tblk, grp):
        buf_ref[GDN_PAD + r0:GDN_PAD + r0 + grp, :] = p_ref[r0:r0 + grp, c0:c0 + 3 * D_GROUP]
        beta = _sigmoid(p_ref[r0:r0 + grp, c0 + 4 * D_GROUP:c0 + 5 * D_GROUP])
        g = nega * _softplus(p_ref[r0:r0 + grp, c0 + 5 * D_GROUP:c0 + 6 * D_GROUP] + dtb)
        acc = None
        for tap in range(SHORT_CONV):
            term = cw[tap:tap + 1, :] * buf_ref[r0 + first + tap:r0 + first + tap + grp, :]
            acc = term if acc is None else acc + term
        qkv = _silu(acc)
        q = qkv[:, 0:D_GROUP]
        k = qkv[:, D_GROUP:2 * D_GROUP]
        v = qkv[:, 2 * D_GROUP:3 * D_GROUP]
        yield
        q = q * lax.rsqrt(_head_sum(q * q, km.bd16[...]) + 1e-6) * HEAD_DIM ** -0.5
        k = k * lax.rsqrt(_head_sum(k * k, km.bd16[...]) + 1e-6)
        G = _dot_exact_lhs(block_tri, g)
        eG = jnp.exp(G)
        kb = k * beta
        vb = (v * beta).astype(BF16)
        kbe = (kb * eG).astype(BF16)
        qeg = q * eG
        kg16 = k.astype(BF16)
        for j in range(0, grp, c):
            Gc.append(G[j:j + c])
            kc.append(k[j:j + c])
            k16.append(kg16[j:j + c])
            vb16.append(vb[j:j + c])
            kbe16.append(kbe[j:j + c])
            qe.append(qeg[j:j + c])
            lhs_aa.append(jnp.concatenate([kb[j:j + c], q[j:j + c]], axis=0).astype(BF16))
        yield
    buf_ref[0:GDN_PAD, :] = buf_ref[tblk:tblk + GDN_PAD, :]

    g_last = [Gc[i][c - 1:c, :] for i in range(nc)]
    L = []
    for i in range(nc):
        g_row = jnp.sum(eye_f * Gc[i], axis=0, keepdims=True)
        L.append(jnp.exp(jnp.minimum(Gc[i] - g_row, 0.0)))
    yield
    aa = [_dot_nt(lhs_aa[i], bd(k16[i])) for i in range(nc)]
    A = [aa[i][0:c] * (L[i] * strict_f) for i in range(nc)]
    attn = [(aa[i][c:2 * c] * (L[i] * incl_f)).astype(BF16) for i in range(nc)]
    yield
    T = [eye_f - A[i] for i in range(nc)]
    P = [_dot(A[i].astype(BF16), bd(A[i])) for i in range(nc)]
    yield
    for it in range(5):
        if it < 4:
            tp = [_dot(jnp.concatenate([T[i], P[i]], axis=0).astype(BF16), bd(P[i])) for i in range(nc)]
            T = [T[i] + tp[i][0:c] for i in range(nc)]
            P = [tp[i][c:2 * c] for i in range(nc)]
        else:
            T = [T[i] + _dot(T[i].astype(BF16), bd(P[i])) for i in range(nc)]
        yield
    T16 = [T[i].astype(BF16) for i in range(nc)]
    u = [_dot(T16[i], bd(vb16[i])) for i in range(nc)]
    w = [_dot(T16[i], bd(kbe16[i])) for i in range(nc)]
    lhs_ws = [jnp.concatenate([w[i], qe[i]], axis=0).astype(BF16) for i in range(nc)]
    k_dec = [(kc[i] * jnp.exp(g_last[i] - Gc[i])).astype(BF16) for i in range(nc)]
    e_last = [jnp.exp(g_last[i]) for i in range(nc)]
    yield
    s = s_ref[...]
    outs = []
    for i in range(nc):
        ws = _dot(lhs_ws[i], s.astype(BF16))
        v_new = u[i] - ws[0:c]
        outs.append(ws[c:2 * c] + _dot(attn[i], bd(v_new)))
        s = s * e_last[i] + _dot_tn(k_dec[i], v_new.astype(BF16)) * km.bd_f[...]
        yield
    s_ref[...] = s
    o = jnp.concatenate(outs, axis=0)
    ms = _head_sum(o * o, km.bd16[...]) * (1.0 / HEAD_DIM)
    y = o * lax.rsqrt(ms + 1e-6) * nw * _silu(p_ref[:, c0 + 3 * D_GROUP:c0 + 4 * D_GROUP])
    o_ref[0, :, oc0:oc0 + D_GROUP] = y.astype(o_ref.dtype)


_GDN_STAGES = 22


def _hgrn_stages(p_ref, c0, log_lb, l1m, oml, nw, o_ref, oc0, st_ref, km, tblk):
    grp = ROW_GROUP
    nsub = grp // SUB
    hsub = SUB // 2
    rr = _iota((grp, grp), 0)
    cc = _iota((grp, grp), 1)
    block_tri = jnp.where((rr // SUB == cc // SUB) & (rr >= cc), 1.0, 0.0).astype(BF16)
    row_in_sub = _iota((nsub, SUB, D_GROUP), 1)
    row_in_half = _iota((nsub, hsub, D_GROUP), 1) + hsub

    bd16_sub = jnp.concatenate([km.bd16[h * HEAD_DIM:h * HEAD_DIM + SUB, :] for h in range(N_HEADS)], axis=0)

    def stack_sub(x16):
        return jnp.concatenate([x16] * N_HEADS, axis=0) * bd16_sub

    groups, lhs_all, kv_all, e_all, cross_all = [], [], [], [], []
    for r0 in range(0, tblk, grp):
        q = p_ref[r0:r0 + grp, c0:c0 + D_GROUP]
        x = p_ref[r0:r0 + grp, c0 + D_GROUP:c0 + 2 * D_GROUP]
        v = p_ref[r0:r0 + grp, c0 + 2 * D_GROUP:c0 + 3 * D_GROUP]
        b_ = l1m + (jnp.minimum(x, 0.0) - _log1pexp(x))
        log_f = jnp.maximum(log_lb, b_) + _log1pexp(log_lb - b_)
        k = oml * _sigmoid(-x)
        G = _dot_exact_lhs(block_tri, log_f)
        G3 = G.reshape(nsub, SUB, D_GROUP)
        k3 = k.reshape(nsub, SUB, D_GROUP)
        groups.append((G3 * LOG2E, q.reshape(nsub, SUB, D_GROUP), k3, v.reshape(nsub, SUB, D_GROUP)))
        eG = jnp.exp(G)
        qg = q * eG
        k_end = (k3 * jnp.exp(G3[:, SUB - 1:SUB, :] - G3)).reshape(grp, D_GROUP)
        v16 = v.astype(BF16)
        ns = HGRN_STATE_CHUNKS
        for lo in range(0, grp, ns * SUB):
            rows = [slice(lo + i * SUB, lo + (i + 1) * SUB) for i in range(ns)]
            e_c = [eG[r.stop - 1:r.stop, :] for r in rows]
            v_st = [stack_sub(v16[r]) for r in rows[:-1]]
            lhs, cross = [qg[rows[0]]], [None]
            before = None
            for i in range(1, ns):
                before = e_c[i - 1] if before is None else before * e_c[i - 1]
                lhs.append(qg[rows[i]] * before)
                k_st, between = [], None
                for j in range(i - 1, -1, -1):
                    kj = k_end[rows[j]] if between is None else k_end[rows[j]] * between
                    k_st.insert(0, stack_sub(kj.astype(BF16)))
                    between = e_c[j] if between is None else between * e_c[j]
                sc = _dot_nt(qg[rows[i]].astype(BF16), jnp.concatenate(k_st, axis=0))
                cross.append(_dot(sc.astype(BF16), jnp.concatenate(v_st[:i], axis=0)))
            k_all, after = [k_end[rows[ns - 1]]], None
            for i in range(ns - 2, -1, -1):
                after = e_c[i + 1] if after is None else after * e_c[i + 1]
                k_all.insert(0, k_end[rows[i]] * after)
            lhs_all.append(jnp.concatenate(lhs, axis=0).astype(BF16))
            cross_all.append(cross)
            kv_all.append(_dot_tn(v16[lo:lo + ns * SUB], jnp.concatenate(k_all, axis=0).astype(BF16))
                          * km.bd_f[...])
            e_all.append(before * e_c[ns - 1])
        yield

    st = st_ref[...]
    inter = []
    for bi in range(len(kv_all)):
        from_state = _dot_nt(lhs_all[bi], st.astype(BF16))
        for i, cross in enumerate(cross_all[bi]):
            part = from_state[i * SUB:(i + 1) * SUB]
            inter.append(part if cross is None else part + cross)
        st = st * e_all[bi] + kv_all[bi]
        if (bi + 1) % max(1, len(kv_all) // 8) == 0:
            yield
    st_ref[...] = st

    intra = []
    for G2, q3, k3, v3 in groups:
        full = None
        for j in range(hsub):
            e = jnp.exp2(jnp.where(row_in_sub >= j, G2 - G2[:, j:j + 1, :], -1e30))
            z = q3 * e * k3[:, j:j + 1, :]
            sc = _head_sum(z.reshape(grp, D_GROUP), km.bd16[...])
            term = sc.reshape(nsub, SUB, D_GROUP) * v3[:, j:j + 1, :]
            full = term if full is None else full + term
            yield
        Gh = G2[:, hsub:SUB, :]
        qh = q3[:, hsub:SUB, :]
        upper = None
        for j in range(hsub, SUB):
            e = jnp.exp2(jnp.where(row_in_half >= j, Gh - G2[:, j:j + 1, :], -1e30))
            z = qh * e * k3[:, j:j + 1, :]
            sc = _head_sum(z.reshape(grp // 2, D_GROUP), km.bd16[...])
            term = sc.reshape(nsub, hsub, D_GROUP) * v3[:, j:j + 1, :]
            upper = term if upper is None else upper + term
            yield
        intra.append(jnp.concatenate([full[:, 0:hsub, :], full[:, hsub:SUB, :] + upper],
                                     axis=1).reshape(grp, D_GROUP))
    o = jnp.concatenate(intra, axis=0) + jnp.concatenate(inter, axis=0)
    ms = _head_sum(o * o, km.bd16[...]) * (1.0 / HEAD_DIM)
    y = o * lax.rsqrt(ms + 1e-6) * nw * _silu(p_ref[:, c0 + 3 * D_GROUP:c0 + 4 * D_GROUP])
    o_ref[0, :, oc0:oc0 + D_GROUP] = y.astype(o_ref.dtype)


_HGRN_STAGES = 43


_C_RET, _C_CONF, _C_GDN, _C_HGRN, _C_END = 0, 4 * D_GROUP, 6 * D_GROUP, 12 * D_GROUP, 16 * D_GROUP
_VEC_ROWS = 16


class _HeadMasks:
    def __init__(self, bd16, bd_f, rope_bd16, rope_state_f):
        self.bd16 = bd16
        self.bd_f = bd_f
        self.rope_bd16 = rope_bd16
        self.rope_state_f = rope_state_f


def _project(x_ref, nmw_ref, w_ref, p_ref, tblk):
    for r0 in range(0, tblk, ROW_GROUP):
        x = x_ref[0, r0:r0 + ROW_GROUP, :]
        h = (x * lax.rsqrt(jnp.mean(x * x, axis=-1, keepdims=True) + 1e-6) * nmw_ref[...]).astype(BF16)
        for c0 in range(0, _C_END, PROJ_CHUNK):
            p_ref[r0:r0 + ROW_GROUP, c0:c0 + PROJ_CHUNK] = _dot(h, w_ref[:, c0:c0 + PROJ_CHUNK])


def _mix_kernel(x_ref, nmw_ref, w_ref, cos_ref, sin_ref, ccw_ref, gcw_ref, vec_ref, o_ref,
                p_ref, cbuf_ref, csh_ref, gbuf_ref, sret_ref, sgdn_ref, shg_ref,
                bd16_ref, bdf_ref, rbd16_ref, rsf_ref, *, tblk):
    @pl.when(pl.program_id(1) == 0)
    def _():
        r = _iota((D_GROUP, D_GROUP), 0)
        l = _iota((D_GROUP, D_GROUP), 1)
        rope_head = lambda i: (i % (D_GROUP // 2)) // (HEAD_DIM // 2)
        bd = jnp.where(r // HEAD_DIM == l // HEAD_DIM, 1.0, 0.0)
        bdf_ref[...] = bd
        bd16_ref[...] = bd.astype(BF16)
        rbd16_ref[...] = jnp.where(r // HEAD_DIM == rope_head(l), 1.0, 0.0).astype(BF16)
        rsf_ref[...] = jnp.where(rope_head(r) == l // HEAD_DIM, 1.0, 0.0)
        sret_ref[...] = jnp.zeros_like(sret_ref)
        sgdn_ref[...] = jnp.zeros_like(sgdn_ref)
        shg_ref[...] = jnp.zeros_like(shg_ref)
        cbuf_ref[0:CONF_PAD, :] = jnp.zeros((CONF_PAD, D_GROUP), F32)
        gbuf_ref[0:GDN_PAD, :] = jnp.zeros((GDN_PAD, 3 * D_GROUP), F32)

    km = _HeadMasks(bd16_ref, bdf_ref, rbd16_ref, rsf_ref)
    _project(x_ref, nmw_ref, w_ref, p_ref, tblk)
    vec = lambda i: vec_ref[i:i + 1, :]
    g = D_GROUP
    _interleave([
        (_ret_stages(p_ref, _C_RET, cos_ref, sin_ref, vec(0), o_ref, 0, sret_ref, km, tblk), _RET_STAGES),
        (_conf_stages(p_ref, _C_CONF, ccw_ref, vec(1), vec(2), vec(3), o_ref, g, cbuf_ref, csh_ref, tblk),
         2 * _CONF_STAGES),
        (_gdn_stages(p_ref, _C_GDN, gcw_ref, vec(4), vec(5), vec(6), o_ref, 2 * g, gbuf_ref, sgdn_ref, km, tblk),
         _GDN_STAGES),
        (_hgrn_stages(p_ref, _C_HGRN, vec(7), vec(8), vec(9), vec(10), o_ref, 3 * g, shg_ref, km, tblk),
         _HGRN_STAGES),
    ])


def _mixers(x, nmw, w, cos, sin, ccw, gcw, vec):
    b, t, _ = x.shape
    tblk = min(MIX_TBLK, t)
    assert t % tblk == 0 and tblk % ROW_GROUP == 0
    const = lambda a: pl.BlockSpec(a.shape, lambda bi, ti: (0, 0), pipeline_mode=pl.Buffered(1))
    timed = lambda a: pl.BlockSpec((tblk, a.shape[1]), lambda bi, ti: (ti, 0))
    blk = pl.BlockSpec((1, tblk, D_MODEL), lambda bi, ti: (bi, ti, 0))
    return pl.pallas_call(
        functools.partial(_mix_kernel, tblk=tblk),
        name="mixers",
        grid=(b, t // tblk),
        in_specs=[blk, const(nmw), const(w), timed(cos), timed(sin), const(ccw), const(gcw), const(vec)],
        out_specs=blk,
        out_shape=jax.ShapeDtypeStruct((b, t, D_MODEL), BF16),
        scratch_shapes=[
            pltpu.VMEM((tblk, _C_END), F32),
            pltpu.VMEM((tblk + CONF_PAD, D_GROUP), F32),
            pltpu.VMEM((7, tblk + CONF_PAD, D_GROUP), F32),
            pltpu.VMEM((tblk + GDN_PAD, 3 * D_GROUP), F32),
            pltpu.VMEM((D_GROUP, D_GROUP), F32),
            pltpu.VMEM((D_GROUP, D_GROUP), F32),
            pltpu.VMEM((D_GROUP, D_GROUP), F32),
            pltpu.VMEM((D_GROUP, D_GROUP), BF16),
            pltpu.VMEM((D_GROUP, D_GROUP), F32),
            pltpu.VMEM((D_GROUP, D_GROUP), BF16),
            pltpu.VMEM((D_GROUP, D_GROUP), F32),
        ],
        compiler_params=pltpu.CompilerParams(
            dimension_semantics=("parallel", "arbitrary"), vmem_limit_bytes=VMEM_LIMIT),
    )(x, nmw, w, cos, sin, ccw, gcw, vec)


def _ffn_kernel(x_ref, m_ref, wo_ref, nfw_ref, wg_ref, wu_ref, wd_ref, nxt_ref, o_ref, *, last):
    y = x_ref[...] + _dot(m_ref[...], wo_ref[...])
    h = (y * lax.rsqrt(jnp.mean(y * y, axis=-1, keepdims=True) + 1e-6) * nfw_ref[...]).astype(BF16)
    down = None
    for f0, f1 in FF_CHUNKS:
        gate = _dot(h, wg_ref[:, f0:f1])
        up = _dot(h, wu_ref[:, f0:f1])
        part = _dot((_silu(gate) * up).astype(BF16), wd_ref[f0:f1, :])
        down = part if down is None else down + part
    y = y + down
    if last:
        y = y * lax.rsqrt(jnp.mean(y * y, axis=-1, keepdims=True) + 1e-6) * nxt_ref[...]
    o_ref[...] = y


def _out_ffn(x2d, mix, wo, nfw, wg, wu, wd, nxt_w, last):
    n = x2d.shape[0]
    tm = min(FFN_TM, n)
    row = pl.BlockSpec((tm, D_MODEL), lambda i: (i, 0))
    const = lambda shape: pl.BlockSpec(shape, lambda i: (0, 0), pipeline_mode=pl.Buffered(1))
    return pl.pallas_call(
        functools.partial(_ffn_kernel, last=last),
        name="out_ffn_last" if last else "out_ffn",
        grid=(n // tm,),
        in_specs=[row, row, const((D_MODEL, D_MODEL)), const((1, D_MODEL)), const((D_MODEL, D_FF)),
                  const((D_MODEL, D_FF)), const((D_FF, D_MODEL)), const((1, D_MODEL))],
        out_specs=row,
        out_shape=jax.ShapeDtypeStruct((n, D_MODEL), F32),
        compiler_params=pltpu.CompilerParams(
            dimension_semantics=("parallel",), vmem_limit_bytes=VMEM_LIMIT),
    )(x2d, mix, wo, nfw, wg, wu, wd, nxt_w)


def _rope_perm(w):
    d = w.shape[0]
    return w.reshape(d, N_HEADS, 2, HEAD_DIM // 2).transpose(0, 2, 1, 3).reshape(d, D_GROUP)


def _per_head_to_lanes(a):
    return jnp.repeat(a, HEAD_DIM, axis=-1)


def kernel(x, norm_mix_w, w_in, ret_norm_w, conf_conv_w, conf_conv_b, conf_ln_w, conf_ln_b, gdn_conv_w, gdn_A_log, gdn_dt_bias, gdn_norm_w, hgrn_lb_logits, hgrn_norm_w, w_out, norm_ffn_w, ffn_w_gate, ffn_w_up, ffn_w_down, final_norm_w):
    b, t, d = x.shape
    n = b * t
    g = D_GROUP
    row = lambda a: a.reshape(1, -1).astype(F32)

    pos = jnp.arange(t, dtype=F32)
    half = HEAD_DIM // 2
    inv = ROPE_BASE ** (-jnp.arange(half, dtype=F32) / half)
    ang = pos[:, None] * inv[None, :]
    cos = jnp.tile(jnp.cos(ang), (1, N_HEADS))
    sin = jnp.tile(jnp.sin(ang), (1, N_HEADS))

    lb_all = jnp.cumsum(jax.nn.softmax(hgrn_lb_logits.astype(F32), axis=0), axis=0)
    lb_all = lb_all - lb_all[0:1]

    for l in range(DEPTH):
        wl = w_in[l]
        o_gb = 10 * g
        w_all = jnp.concatenate([
            _rope_perm(wl[:, 0:g]), _rope_perm(wl[:, g:2 * g]), wl[:, 2 * g:4 * g],
            wl[:, 4 * g:6 * g],
            wl[:, 6 * g:10 * g], _per_head_to_lanes(wl[:, o_gb:o_gb + N_HEADS]),
            _per_head_to_lanes(wl[:, o_gb + N_HEADS:o_gb + 2 * N_HEADS]),
            wl[:, o_gb + 2 * N_HEADS:],
        ], axis=1).astype(BF16)
        lb = lb_all[l]
        vec = jnp.concatenate([
            row(ret_norm_w[l]), row(conf_conv_b[l]), row(conf_ln_w[l]), row(conf_ln_b[l]),
            row(_per_head_to_lanes(-jnp.exp(gdn_A_log[l].astype(F32)))),
            row(_per_head_to_lanes(gdn_dt_bias[l].astype(F32))),
            row(jnp.tile(gdn_norm_w[l], N_HEADS)),
            row(jnp.log(lb)), row(jnp.log1p(-lb)), row(1.0 - lb), row(jnp.tile(hgrn_norm_w[l], N_HEADS)),
            jnp.zeros((_VEC_ROWS - 11, g), F32)], axis=0)
        ccw = jnp.concatenate([conf_conv_w[l].astype(F32), jnp.zeros((CONF_PAD - CONF_KERNEL, g), F32)], axis=0)
        gcw = jnp.concatenate([gdn_conv_w[l].astype(F32), jnp.zeros((GDN_PAD - SHORT_CONV, 3 * g), F32)], axis=0)
        mix = _mixers(x, row(norm_mix_w[l]), w_all, cos, sin, ccw, gcw, vec)

        last = l == DEPTH - 1
        x = _out_ffn(x.reshape(n, d), mix.reshape(n, d), w_out[l].astype(BF16), row(norm_ffn_w[l]),
                     ffn_w_gate[l].astype(BF16), ffn_w_up[l].astype(BF16), ffn_w_down[l].astype(BF16),
                     row(final_norm_w), last).reshape(b, t, d)
    return x
```
